```python
import jax
import jax.numpy as jnp
from jax import lax
import numpy as np

D_MODEL = 1024
BATCH = 4
SEQ = 8192
DEPTH = 2

CHUNK = 64
Q_BLOCK = 128
HEAD_DIM = 64
H_SPARSE = 8
H_FOX = 8
H_IDX = 8
D_IDX = 64
TOPK_MAX = 256
ROPE_THETA = 500000.0
ROT_DIM = HEAD_DIM // 4
ROT_DIM_IDX = D_IDX // 4
D_FF = 2816
CONV_W = 3
N_BRANCH = 2
EPS = 1e-6
NEG_INF = -1e30

W_SPARSE = H_SPARSE * HEAD_DIM
W_FOX = H_FOX * HEAD_DIM
IN_SPLITS = (W_SPARSE, HEAD_DIM, HEAD_DIM, H_IDX * D_IDX, D_IDX, H_IDX,
             W_FOX, W_FOX, W_FOX, H_FOX, N_BRANCH * D_MODEL)
N_IN = W_SPARSE + 2 * HEAD_DIM + H_IDX * D_IDX + D_IDX + H_IDX + 3 * W_FOX + H_FOX + N_BRANCH * D_MODEL

kernel_name = "hybrid_dsa_fox_convffn_adaln"


def _rmsnorm(x, g):
    xf = x.astype(jnp.float32)
    y = xf * lax.rsqrt(jnp.mean(xf * xf, axis=-1, keepdims=True) + EPS)
    return (y * g.astype(jnp.float32)).astype(x.dtype)


def _partial_rope(x, positions, rot_dim):
    half = rot_dim // 2
    inv_freq = ROPE_THETA ** (-jnp.arange(0, rot_dim, 2, dtype=jnp.float32) / rot_dim)
    ang = positions.astype(jnp.float32)[..., None] * inv_freq
    cos = jnp.cos(ang)[:, :, None, :]
    sin = jnp.sin(ang)[:, :, None, :]
    xf = x.astype(jnp.float32)
    x1, x2, rest = xf[..., :half], xf[..., half:rot_dim], xf[..., rot_dim:]
    out = jnp.concatenate([x1 * cos - x2 * sin, x2 * cos + x1 * sin, rest], axis=-1)
    return out.astype(x.dtype)


def _to_blocks(a, nb):
    b = a.shape[0]
    return jnp.moveaxis(a.reshape((b, nb, Q_BLOCK) + a.shape[2:]), 1, 0)


def _from_blocks(a):
    a = jnp.moveaxis(a, 0, 1)
    return a.reshape((a.shape[0], a.shape[1] * a.shape[2]) + a.shape[3:])


def _sparse_attention(q, k, v, q_idx, k_idx, w_idx):
    S = q.shape[1]
    nb = S // Q_BLOCK
    topk = min(TOPK_MAX, S // 4)
    key_chunk = jnp.arange(S) // CHUNK
    idx_scale = D_IDX ** -0.5
    att_scale = HEAD_DIM ** -0.5

    def block(args):
        qb, qib, wb, q_start = args
        q_chunk = (q_start + jnp.arange(Q_BLOCK)) // CHUNK
        admissible = key_chunk[None, :] <= q_chunk[:, None]
        s = jnp.einsum("bqhd,bsd->bqhs", qib, k_idx).astype(jnp.float32) * idx_scale
        score = jnp.einsum("bqh,bqhs->bqs", wb.astype(jnp.float32), jax.nn.relu(s))
        score = jnp.where(admissible[None], score, NEG_INF)
        top_val, top_idx = lax.top_k(score, topk)
        valid = top_val > 0.5 * NEG_INF
        k_sel = jax.vmap(lambda kk, ii: kk[ii])(k, top_idx)
        v_sel = jax.vmap(lambda vv, ii: vv[ii])(v, top_idx)
        logits = jnp.einsum("bqhd,bqkd->bqhk", qb, k_sel).astype(jnp.float32) * att_scale
        logits = jnp.where(valid[:, :, None, :], logits, NEG_INF)
        p = jax.nn.softmax(logits, axis=-1).astype(v.dtype)
        return jnp.einsum("bqhk,bqkd->bqhd", p, v_sel)

    starts = jnp.arange(nb) * Q_BLOCK
    out = lax.map(block, (_to_blocks(q, nb), _to_blocks(q_idx, nb), _to_blocks(w_idx, nb), starts))
    return _from_blocks(out)


def _forgetting_attention(q, k, v, log_f):
    S = q.shape[1]
    nb = S // Q_BLOCK
    cum = jnp.cumsum(log_f, axis=1)
    cum_k = jnp.moveaxis(cum, 1, 2)
    key_pos = jnp.arange(S)
    scale = HEAD_DIM ** -0.5

    def block(args):
        qb, cq, q_start = args
        causal = key_pos[None, :] <= (q_start + jnp.arange(Q_BLOCK))[:, None]
        logits = jnp.einsum("bqhd,bshd->bhqs", qb, k).astype(jnp.float32) * scale
        logits = logits + jnp.moveaxis(cq, 1, 2)[..., None] - cum_k[:, :, None, :]
        logits = jnp.where(causal[None, None], logits, NEG_INF)
        p = jax.nn.softmax(logits, axis=-1).astype(v.dtype)
        return jnp.einsum("bhqs,bshd->bqhd", p, v)

    starts = jnp.arange(nb) * Q_BLOCK
    out = lax.map(block, (_to_blocks(q, nb), _to_blocks(cum, nb), starts))
    return _from_blocks(out)


def _mixer(h, positions, w_in, forget_bias, w_branch_a, w_branch_b, w_out):
    B, S, _ = h.shape
    proj = h @ w_in
    offs = np.cumsum(IN_SPLITS)[:-1].tolist()
    q_a, k_a, v_a, q_i, k_i, w_i, q_f, k_f, v_f, f_f, gates = jnp.split(proj, offs, axis=-1)
    q_a = _partial_rope(q_a.reshape(B, S, H_SPARSE, HEAD_DIM), positions, ROT_DIM)
    k_a = _partial_rope(k_a[:, :, None, :], positions, ROT_DIM)[:, :, 0]
    q_i = _partial_rope(q_i.reshape(B, S, H_IDX, D_IDX), positions, ROT_DIM_IDX)
    k_i = _partial_rope(k_i[:, :, None, :], positions, ROT_DIM_IDX)[:, :, 0]
    w_i = w_i * (H_IDX ** -0.5)
    o_a = _sparse_attention(q_a, k_a, v_a, q_i, k_i, w_i)
    y_a = o_a.reshape(B, S, W_SPARSE) @ w_branch_a
    log_f = jax.nn.log_sigmoid(f_f.astype(jnp.float32) + forget_bias.astype(jnp.float32))
    o_b = _forgetting_attention(q_f.reshape(B, S, H_FOX, HEAD_DIM),
                                k_f.reshape(B, S, H_FOX, HEAD_DIM),
                                v_f.reshape(B, S, H_FOX, HEAD_DIM), log_f)
    y_b = o_b.reshape(B, S, W_FOX) @ w_branch_b
    g_a, g_b = jnp.split(jax.nn.sigmoid(gates.astype(jnp.float32)).astype(h.dtype), N_BRANCH, axis=-1)
    return (g_a * y_a + g_b * y_b) @ w_out


def _conv_ffn(h, w_up, conv_w, conv_b, w_down):
    S = h.shape[1]
    u = h @ w_up
    up = jnp.pad(u, ((0, 0), (CONV_W - 1, 0), (0, 0)))
    u = conv_b + sum(conv_w[i] * up[:, i:i + S] for i in range(CONV_W))
    a, b = jnp.split(u, 2, axis=-1)
    return (jax.nn.silu(a) * b) @ w_down


def setup_inputs(seed: int = 0) -> dict:
    key = jax.random.key(seed)
    ks = jax.random.split(key, 17)
    f32 = jnp.float32

    def nrm(k, shape, fan_in, mult=1.0):
        return jax.random.normal(k, shape, f32) * (mult * fan_in ** -0.5)

    x = jax.random.normal(ks[0], (BATCH, SEQ, D_MODEL), f32)
    c = jax.random.normal(ks[1], (BATCH, D_MODEL), f32)
    offset = jax.random.randint(ks[2], (BATCH, 1), 0, 4096, dtype=jnp.int32)
    positions = offset + jnp.arange(SEQ, dtype=jnp.int32)[None, :]
    mod_w = nrm(ks[3], (DEPTH, D_MODEL, 6 * D_MODEL), D_MODEL, 0.5)
    mod_b = 0.01 * jax.random.normal(ks[4], (DEPTH, 6 * D_MODEL), f32)
    norm1_g = 1.0 + 0.02 * jax.random.normal(ks[5], (DEPTH, D_MODEL), f32)
    norm2_g = 1.0 + 0.02 * jax.random.normal(ks[6], (DEPTH, D_MODEL), f32)
    w_in = nrm(ks[7], (DEPTH, D_MODEL, N_IN), D_MODEL)
    forget_bias = 4.0 + 0.5 * jax.random.normal(ks[8], (DEPTH, H_FOX), f32)
    w_branch_a = nrm(ks[9], (DEPTH, W_SPARSE, D_MODEL), W_SPARSE)
    w_branch_b = nrm(ks[10], (DEPTH, W_FOX, D_MODEL), W_FOX)
    w_out = nrm(ks[11], (DEPTH, D_MODEL, D_MODEL), D_MODEL)
    w_up = nrm(ks[12], (DEPTH, D_MODEL, 2 * D_FF), D_MODEL)
    conv_w = nrm(ks[13], (DEPTH, CONV_W, 2 * D_FF), CONV_W)
    conv_b = 0.01 * jax.random.normal(ks[14], (DEPTH, 2 * D_FF), f32)
    w_down = nrm(ks[15], (DEPTH, D_FF, D_MODEL), D_FF)
    final_g = 1.0 + 0.02 * jax.random.normal(ks[16], (D_MODEL,), f32)
    return {"x": x, "c": c, "positions": positions, "mod_w": mod_w, "mod_b": mod_b,
            "norm1_g": norm1_g, "norm2_g": norm2_g, "w_in": w_in, "forget_bias": forget_bias,
            "w_branch_a": w_branch_a, "w_branch_b": w_branch_b, "w_out": w_out,
            "w_up": w_up, "conv_w": conv_w, "conv_b": conv_b, "w_down": w_down,
            "final_g": final_g}


def reference(x, c, positions, mod_w, mod_b, norm1_g, norm2_g, w_in, forget_bias,
              w_branch_a, w_branch_b, w_out, w_up, conv_w, conv_b, w_down, final_g):
    for l in range(DEPTH):
        mod = c @ mod_w[l] + mod_b[l]
        sh1, sc1, g1, sh2, sc2, g2 = [m[:, None, :] for m in jnp.split(mod, 6, axis=-1)]
        h = _rmsnorm(x, norm1_g[l]) * (1.0 + sc1) + sh1
        x = x + g1 * _mixer(h, positions, w_in[l], forget_bias[l], w_branch_a[l], w_branch_b[l], w_out[l])
        h = _rmsnorm(x, norm2_g[l]) * (1.0 + sc2) + sh2
        x = x + g2 * _conv_ffn(h, w_up[l], conv_w[l], conv_b[l], w_down[l])
    return _rmsnorm(x, final_g)
```

```python
import functools

import numpy as np
import jax
import jax.numpy as jnp
from jax import lax
from jax.experimental import pallas as pl
from jax.experimental.pallas import tpu as pltpu

F32 = jnp.float32
BF16 = jnp.bfloat16
I32 = jnp.int32

D_MODEL = 1024
DEPTH = 2
CHUNK = 64
HEAD_DIM = 64
N_HEADS = 8
TOPK_MAX = 256
ROPE_THETA = 500000.0
ROT_DIM = HEAD_DIM // 4
D_FF = 2816
EPS = 1e-6
NEG_INF = -1e30
W_HEADS = N_HEADS * HEAD_DIM

LANES = 128
SUBLANES = 8
VMEM_LIMIT = 56 * 1024 * 1024

COL_QA = 0
COL_QI = 512
COL_QF = 1024
COL_KF = 1536
COL_VF = 2048
COL_GATES = 2560
COL_SMALL = 4608
N_COLS = 4864
LANE_WIDX = 64
LANE_FORGET = 72

INT_MIN = -(2 ** 31)
M_INIT = -3.0e38


def _f32_key(v):
    b = int(np.float32(v).view(np.int32))
    return b ^ ((b >> 31) & 0x7FFFFFFF)


MIN_VALID_KEY = _f32_key(0.5 * NEG_INF) + 1


def _params(n_grid):
    return pltpu.CompilerParams(
        dimension_semantics=("arbitrary",) * n_grid, vmem_limit_bytes=VMEM_LIMIT)


def _mod_kernel(c_ref, w_ref, b_ref, o_ref):
    o_ref[...] = jnp.dot(c_ref[...], w_ref[...], precision=lax.Precision.HIGHEST,
                         preferred_element_type=F32) + b_ref[...]


def _mod_call(c, mod_w, mod_b):
    depth = mod_w.shape[0]
    c8 = jnp.zeros((SUBLANES, D_MODEL), F32).at[: c.shape[0]].set(c)
    b4 = mod_b.reshape(depth, 6, 1, D_MODEL)
    return pl.pallas_call(
        _mod_kernel,
        grid=(depth, 6),
        in_specs=[
            pl.BlockSpec((SUBLANES, D_MODEL), lambda l, j: (0, 0)),
            pl.BlockSpec((None, D_MODEL, D_MODEL), lambda l, j: (l, 0, j)),
            pl.BlockSpec((None, None, 1, D_MODEL), lambda l, j: (l, j, 0, 0)),
        ],
        out_specs=pl.BlockSpec((None, None, SUBLANES, D_MODEL), lambda l, j: (l, j, 0, 0)),
        out_shape=jax.ShapeDtypeStruct((depth, 6, SUBLANES, D_MODEL), F32),
        compiler_params=_params(2),
        name="mod",
    )(c8, mod_w, b4)


def _rms_mod(x, gain, scale, shift):
    var = jnp.mean(x * x, axis=-1, keepdims=True)
    y = x * lax.rsqrt(var + EPS) * gain
    return y * (1.0 + scale) + shift


def _in_kernel(x_ref, sh_ref, sc_ref, g_ref, pos_ref, invf_ref, fb_ref, w_ref,
               qa_ref, qi_ref, qf_ref, kf_ref, vf_ref, ga_ref, gb_ref,
               kat_ref, kit_ref, va_ref, small_ref, cumt_ref, carry_scr, *, tm):
    s_idx = pl.program_id(1)

    @pl.when(s_idx == 0)
    def _():
        carry_scr[...] = jnp.zeros_like(carry_scr)

    h = _rms_mod(x_ref[...], g_ref[...], sc_ref[...], sh_ref[...])
    hb = h.astype(BF16)

    ang = pos_ref[...] * invf_ref[...]
    cos = jnp.cos(ang)
    sin = jnp.sin(ang)
    l64 = lax.broadcasted_iota(I32, (tm, LANES), 1) & (HEAD_DIM - 1)
    half = ROT_DIM // 2
    t_cos = jnp.where(l64 < ROT_DIM, cos, 1.0)
    t_lo = jnp.where(l64 < half, -sin, 0.0)
    t_hi = jnp.where(l64 < half, 0.0, jnp.where(l64 < ROT_DIM, sin, 0.0))

    def rope(xc):
        return (xc * t_cos + pltpu.roll(xc, LANES - half, 1) * t_lo
                + pltpu.roll(xc, half, 1) * t_hi)

    def proj(col, width):
        return jnp.dot(hb, w_ref[:, col:col + width], preferred_element_type=F32)

    for col, out_ref in ((COL_QA, qa_ref), (COL_QI, qi_ref)):
        p = proj(col, W_HEADS)
        for cblk in range(W_HEADS // LANES):
            r = (rope(p[:, cblk * LANES:(cblk + 1) * LANES]) * 0.125).astype(BF16)
            out_ref[2 * cblk] = r[:, :HEAD_DIM]
            out_ref[2 * cblk + 1] = r[:, HEAD_DIM:]

    qf_ref[...] = (proj(COL_QF, W_HEADS) * 0.125).astype(BF16)
    kf_ref[...] = proj(COL_KF, W_HEADS).astype(BF16)
    vf_ref[...] = proj(COL_VF, W_HEADS).astype(BF16)
    ga_ref[...] = jax.nn.sigmoid(proj(COL_GATES, D_MODEL)).astype(BF16)
    gb_ref[...] = jax.nn.sigmoid(proj(COL_GATES + D_MODEL, D_MODEL)).astype(BF16)

    small = proj(COL_SMALL, 2 * LANES)
    keys_t = rope(small[:, :LANES]).T.astype(BF16)
    kat_ref[...] = keys_t[:HEAD_DIM]
    kit_ref[...] = keys_t[HEAD_DIM:]
    blk = small[:, LANES:]
    va_ref[...] = blk[:, :HEAD_DIM].astype(BF16)

    lane = lax.broadcasted_iota(I32, (tm, LANES), 1)
    is_f = (lane >= LANE_FORGET) & (lane < LANE_FORGET + N_HEADS)
    z = blk + fb_ref[...]
    logf = jnp.minimum(z, 0.0) - jnp.log1p(jnp.exp(-jnp.abs(z)))
    logf = jnp.where(is_f, logf, 0.0)
    tri = (lax.broadcasted_iota(I32, (tm, tm), 0)
           >= lax.broadcasted_iota(I32, (tm, tm), 1)).astype(BF16)
    p0 = logf.astype(BF16)
    r1 = logf - p0.astype(F32)
    p1 = r1.astype(BF16)
    p2 = (r1 - p1.astype(F32)).astype(BF16)
    cum = (jnp.dot(tri, p0, preferred_element_type=F32)
           + jnp.dot(tri, p1, preferred_element_type=F32)
           + jnp.dot(tri, p2, preferred_element_type=F32)) + carry_scr[0:1, :]
    carry_scr[0:1, :] = cum[tm - 1:tm, :]

    is_w = (lane >= LANE_WIDX) & (lane < LANE_WIDX + N_HEADS)
    small_ref[...] = jnp.where(is_w, blk * (N_HEADS ** -0.5), jnp.where(is_f, cum, 0.0))
    cumt_ref[...] = cum.T[LANE_FORGET:LANE_FORGET + N_HEADS, :]


def _in_call(x, sh, sc, gain, pos, invf, fb, w_all, *, tm):
    B, S, _ = x.shape
    hm = jax.ShapeDtypeStruct((B, N_HEADS, S, HEAD_DIM), BF16)
    wide = jax.ShapeDtypeStruct((B, S, W_HEADS), BF16)
    gate = jax.ShapeDtypeStruct((B, S, D_MODEL), BF16)
    kt = jax.ShapeDtypeStruct((B, HEAD_DIM, S), BF16)
    row = lambda b, s: (b, s, 0)
    vec = pl.BlockSpec((None, 1, D_MODEL), lambda b, s: (b, 0, 0))
    const2 = lambda b, s: (0, 0)
    hm_spec = pl.BlockSpec((None, N_HEADS, tm, HEAD_DIM), lambda b, s: (b, 0, s, 0))
    return pl.pallas_call(
        functools.partial(_in_kernel, tm=tm),
        grid=(B, S // tm),
        in_specs=[
            pl.BlockSpec((None, tm, D_MODEL), row),
            vec, vec,
            pl.BlockSpec((1, D_MODEL), const2),
            pl.BlockSpec((None, tm, 1), row),
            pl.BlockSpec((1, LANES), const2),
            pl.BlockSpec((1, LANES), const2),
            pl.BlockSpec((D_MODEL, N_COLS), const2),
        ],
        out_specs=[
            hm_spec, hm_spec,
            pl.BlockSpec((None, tm, W_HEADS), row),
            pl.BlockSpec((None, tm, W_HEADS), row),
            pl.BlockSpec((None, tm, W_HEADS), row),
            pl.BlockSpec((None, tm, D_MODEL), row),
            pl.BlockSpec((None, tm, D_MODEL), row),
            pl.BlockSpec((None, HEAD_DIM, tm), lambda b, s: (b, 0, s)),
            pl.BlockSpec((None, HEAD_DIM, tm), lambda b, s: (b, 0, s)),
            pl.BlockSpec((None, tm, HEAD_DIM), row),
            pl.BlockSpec((None, tm, LANES), row),
            pl.BlockSpec((None, N_HEADS, tm), lambda b, s: (b, 0, s)),
        ],
        out_shape=[hm, hm, wide, wide, wide, gate, gate, kt, kt,
                   jax.ShapeDtypeStruct((B, S, HEAD_DIM), BF16),
                   jax.ShapeDtypeStruct((B, S, LANES), F32),
                   jax.ShapeDtypeStruct((B, N_HEADS, S), F32)],
        scratch_shapes=[pltpu.VMEM((SUBLANES, LANES), F32)],
        compiler_params=_params(2),
        name="in_proj",
    )(x, sh, sc, gain, pos, invf, fb, w_all)


def _dsa_kernel(qi_ref, qa_ref, small_ref, kit_ref, kat_ref, va_ref, place_ref, o_ref,
                keys_scr, m_scr, l_scr, acc_scr, p_scr, *, q_blk, tk, topk, idx_bits):
    i = pl.program_id(1)
    rows = N_HEADS * q_blk
    n_tiles = ((i + 1) * q_blk + tk - 1) // tk
    n_chunks = tk // LANES

    qi = qi_ref[...].reshape(rows, HEAD_DIM)
    qa = qa_ref[...].reshape(rows, HEAD_DIM)
    w_idx = small_ref[:, LANE_WIDX:LANE_WIDX + N_HEADS]
    w_b = [jnp.broadcast_to(w_idx[:, h:h + 1], (q_blk, LANES)) for h in range(N_HEADS)]
    q_row = lax.broadcasted_iota(I32, (q_blk, LANES), 0)
    key_lim = ((i * q_blk + q_row) // CHUNK + 1) * CHUNK
    lane = lax.broadcasted_iota(I32, (q_blk, LANES), 1)

    def score_tile(t, _):
        off = pl.multiple_of(t * tk, tk)
        s = jnp.dot(qi, kit_ref[:, pl.ds(off, tk)], preferred_element_type=F32)
        for cblk in range(n_chunks):
            acc = jnp.zeros((q_blk, LANES), F32)
            for h in range(N_HEADS):
                sh = s[h * q_blk:(h + 1) * q_blk, cblk * LANES:(cblk + 1) * LANES]
                acc = acc + w_b[h] * jnp.maximum(sh, 0.0)
            bits = pltpu.bitcast(acc, I32)
            key = bits ^ ((bits >> 31) & 0x7FFFFFFF)
            kidx = off + cblk * LANES + lane
            key = jnp.where(kidx < key_lim, key, INT_MIN)
            keys_scr[:, pl.ds(off + cblk * LANES, LANES)] = key
        return 0

    lax.fori_loop(0, n_tiles, score_tile, 0)

    def count(pred):
        def tile(t, acc):
            off = pl.multiple_of(t * tk, tk)
            for cblk in range(n_chunks):
                k = keys_scr[:, pl.ds(off + cblk * LANES, LANES)]
                acc = acc + pred(k, off + cblk * LANES + lane)
            return acc
        acc = lax.fori_loop(0, n_tiles, tile, jnp.zeros((q_blk, LANES), I32))
        return jnp.broadcast_to(jnp.sum(acc, axis=1, keepdims=True), (q_blk, LANES))

    def radix_pass(b, carry):
        prefix, cnt_ge = carry
        cand = prefix | jnp.left_shift(jnp.int32(1), 31 - b)
        cand_s = cand ^ INT_MIN
        cnt = count(lambda k, _: jnp.where(k >= cand_s, 1, 0))
        ok = cnt >= topk
        return jnp.where(ok, cand, prefix), jnp.where(ok, cnt, cnt_ge)

    prefix0 = jnp.zeros((q_blk, LANES), I32)
    total = jnp.zeros((q_blk, LANES), I32) + n_tiles * tk
    prefix, cnt_ge = lax.fori_loop(0, 32, radix_pass, (prefix0, total))
    kth = prefix ^ INT_MIN
    thr = jnp.maximum(kth, MIN_VALID_KEY)

    has_tie = jnp.where((cnt_ge > topk) & (kth >= MIN_VALID_KEY), 1, 0)

    @pl.when(jnp.max(has_tie) > 0)
    def _():
        n_gt = count(lambda k, _: jnp.where(k > thr, 1, 0))
        need = topk - n_gt

        def idx_pass(b, pre):
            cand = pre | jnp.left_shift(jnp.int32(1), idx_bits - 1 - b)
            c = count(lambda k, kidx: jnp.where(k == thr, jnp.where(kidx < cand, 1, 0), 0))
            return jnp.where(c < need, cand, pre)

        last = lax.fori_loop(0, idx_bits, idx_pass, jnp.zeros((q_blk, LANES), I32))

        def drop_tile(t, _):
            off = pl.multiple_of(t * tk, tk)
            for cblk in range(n_chunks):
                sl = pl.ds(off + cblk * LANES, LANES)
                k = keys_scr[:, sl]
                kidx = off + cblk * LANES + lane
                keys_scr[:, sl] = jnp.where(k == thr, jnp.where(kidx > last, INT_MIN, k), k)
            return 0

        lax.fori_loop(0, n_tiles, drop_tile, 0)

    m_scr[...] = jnp.full_like(m_scr, M_INIT)
    l_scr[...] = jnp.zeros_like(l_scr)
    acc_scr[...] = jnp.zeros_like(acc_scr)

    def attn_tile(t, _):
        off = pl.multiple_of(t * tk, tk)
        logits = jnp.dot(qa, kat_ref[:, pl.ds(off, tk)], preferred_element_type=F32)
        keys = keys_scr[:, pl.ds(off, tk)]
        bias = jnp.where(keys >= jnp.tile(thr, (1, n_chunks)), 0.0, NEG_INF)
        for h in range(N_HEADS):
            r = slice(h * q_blk, (h + 1) * q_blk)
            s = logits[r] + bias
            m_prev = m_scr[r]
            m_new = jnp.maximum(m_prev, jnp.max(s, axis=1, keepdims=True))
            p = jnp.exp(s - jnp.tile(m_new, (1, n_chunks)))
            alpha = jnp.exp(m_prev - m_new)
            l_scr[r] = alpha * l_scr[r] + jnp.sum(p, axis=1, keepdims=True)
            m_scr[r] = m_new
            acc_scr[r] = acc_scr[r] * alpha[:, :HEAD_DIM]
            p_scr[r] = p.astype(BF16)
        acc_scr[...] += jnp.dot(p_scr[...], va_ref[pl.ds(off, tk), :], preferred_element_type=F32)
        return 0

    lax.fori_loop(0, n_tiles, attn_tile, 0)

    o = (acc_scr[...] / l_scr[:, :HEAD_DIM]).astype(BF16)
    out = jnp.zeros((q_blk, W_HEADS), F32)
    for h in range(N_HEADS):
        out = out + jnp.dot(o[h * q_blk:(h + 1) * q_blk], place_ref[h],
                            preferred_element_type=F32)
    o_ref[...] = out.astype(BF16)


def _dsa_call(qi, qa, small, kit, kat, va, place, *, q_blk, tk):
    B, _, S, _ = qi.shape
    topk = min(TOPK_MAX, S // 4)
    idx_bits = max(1, int(np.ceil(np.log2(S))))
    rows = N_HEADS * q_blk
    hm_spec = pl.BlockSpec((None, N_HEADS, q_blk, HEAD_DIM), lambda b, i: (b, 0, i, 0))
    seq_t = pl.BlockSpec((None, HEAD_DIM, S), lambda b, i: (b, 0, 0))
    return pl.pallas_call(
        functools.partial(_dsa_kernel, q_blk=q_blk, tk=tk, topk=topk, idx_bits=idx_bits),
        grid=(B, S // q_blk),
        in_specs=[
            hm_spec, hm_spec,
            pl.BlockSpec((None, q_blk, LANES), lambda b, i: (b, i, 0)),
            seq_t, seq_t,
            pl.BlockSpec((None, S, HEAD_DIM), lambda b, i: (b, 0, 0)),
            pl.BlockSpec((N_HEADS, HEAD_DIM, W_HEADS), lambda b, i: (0, 0, 0)),
        ],
        out_specs=pl.BlockSpec((None, q_blk, W_HEADS), lambda b, i: (b, i, 0)),
        out_shape=jax.ShapeDtypeStruct((B, S, W_HEADS), BF16),
        scratch_shapes=[
            pltpu.VMEM((q_blk, S), I32),
            pltpu.VMEM((rows, LANES), F32),
            pltpu.VMEM((rows, LANES), F32),
            pltpu.VMEM((rows, HEAD_DIM), F32),
            pltpu.VMEM((rows, tk), BF16),
        ],
        compiler_params=_params(2),
        name="dsa",
    )(qi, qa, small, kit, kat, va, place)


def _fox_kernel(qi_tbl, kj_tbl, q_ref, k_ref, v_ref, cq_ref, ckt_ref, o_ref,
                m_scr, l_scr, acc_scr, *, t):
    step = pl.program_id(1)
    i = qi_tbl[step]
    j = kj_tbl[step]
    n_chunks = t // LANES
    lane = lax.broadcasted_iota(I32, (t, LANES), 1)
    low = lane < HEAD_DIM

    @pl.when(j == 0)
    def _():
        m_scr[...] = jnp.full_like(m_scr, M_INIT)
        l_scr[...] = jnp.zeros_like(l_scr)
        acc_scr[...] = jnp.zeros_like(acc_scr)

    def body(diag):
        cq = cq_ref[...]
        ck = ckt_ref[...]
        if diag:
            causal = (lax.broadcasted_iota(I32, (t, t), 0)
                      >= lax.broadcasted_iota(I32, (t, t), 1))
        for pair in range(N_HEADS // 2):
            cols = slice(pair * LANES, (pair + 1) * LANES)
            qp = q_ref[:, cols]
            kp = k_ref[:, cols]
            vp = v_ref[:, cols]
            alphas, pvs = [], []
            for e in range(2):
                h = 2 * pair + e
                qm = jnp.where(low, qp, 0) if e == 0 else jnp.where(low, 0, qp)
                s = lax.dot_general(qm, kp, (((1,), (1,)), ((), ())),
                                    preferred_element_type=F32)
                s = s + cq[:, LANE_FORGET + h:LANE_FORGET + h + 1] - ck[h:h + 1, :]
                if diag:
                    s = jnp.where(causal, s, NEG_INF)
                m_prev = m_scr[h]
                m_new = jnp.maximum(m_prev, jnp.max(s, axis=1, keepdims=True))
                p = jnp.exp(s - jnp.tile(m_new, (1, n_chunks)))
                alpha = jnp.exp(m_prev - m_new)
                l_scr[h] = alpha * l_scr[h] + jnp.sum(p, axis=1, keepdims=True)
                m_scr[h] = m_new
                alphas.append(alpha)
                pvs.append(jnp.dot(p.astype(BF16), vp, preferred_element_type=F32))
            acc_scr[pair] = (acc_scr[pair] * jnp.where(low, alphas[0], alphas[1])
                             + jnp.where(low, pvs[0], pvs[1]))

    @pl.when(j < i)
    def _():
        body(False)

    @pl.when(j == i)
    def _():
        body(True)
        for pair in range(N_HEADS // 2):
            denom = jnp.where(low, l_scr[2 * pair], l_scr[2 * pair + 1])
            o_ref[:, pair * LANES:(pair + 1) * LANES] = (acc_scr[pair] / denom).astype(BF16)


def _fox_call(q, k, v, small, cumt, *, t):
    B, S, _ = q.shape
    n = S // t
    qi_tbl = np.concatenate([np.full(i + 1, i, np.int32) for i in range(n)])
    kj_tbl = np.concatenate([np.arange(i + 1, dtype=np.int32) for i in range(n)])
    grid_spec = pltpu.PrefetchScalarGridSpec(
        num_scalar_prefetch=2,
        grid=(B, len(qi_tbl)),
        in_specs=[
            pl.BlockSpec((None, t, W_HEADS), lambda b, s, qi, kj: (b, qi[s], 0)),
            pl.BlockSpec((None, t, W_HEADS), lambda b, s, qi, kj: (b, kj[s], 0)),
            pl.BlockSpec((None, t, W_HEADS), lambda b, s, qi, kj: (b, kj[s], 0)),
            pl.BlockSpec((None, t, LANES), lambda b, s, qi, kj: (b, qi[s], 0)),
            pl.BlockSpec((None, N_HEADS, t), lambda b, s, qi, kj: (b, 0, kj[s])),
        ],
        out_specs=pl.BlockSpec((None, t, W_HEADS), lambda b, s, qi, kj: (b, qi[s], 0)),
        scratch_shapes=[
            pltpu.VMEM((N_HEADS, t, LANES), F32),
            pltpu.VMEM((N_HEADS, t, LANES), F32),
            pltpu.VMEM((N_HEADS // 2, t, LANES), F32),
        ],
    )
    return pl.pallas_call(
        functools.partial(_fox_kernel, t=t),
        grid_spec=grid_spec,
        out_shape=jax.ShapeDtypeStruct((B, S, W_HEADS), BF16),
        compiler_params=_params(2),
        name="fox",
    )(jnp.asarray(qi_tbl), jnp.asarray(kj_tbl), q, k, v, small, cumt)


def _merge_kernel(oa_ref, ob_ref, ga_ref, gb_ref, x_ref, g1_ref, sc_ref, sh_ref, n2_ref,
                  wa_ref, wb_ref, wo_ref, x1_ref, h2_ref):
    ya = jnp.dot(oa_ref[...], wa_ref[...], preferred_element_type=F32)
    yb = jnp.dot(ob_ref[...], wb_ref[...], preferred_element_type=F32)
    mix = ga_ref[...].astype(F32) * ya + gb_ref[...].astype(F32) * yb
    y = jnp.dot(mix.astype(BF16), wo_ref[...], preferred_element_type=F32)
    x1 = x_ref[...] + g1_ref[...] * y
    x1_ref[...] = x1
    h2_ref[...] = _rms_mod(x1, n2_ref[...], sc_ref[...], sh_ref[...]).astype(BF16)


def _merge_call(oa, ob, ga, gb, x, g1, sc2, sh2, n2, wa, wb, wo, *, tm):
    B, S, _ = x.shape
    row = lambda b, s: (b, s, 0)
    const2 = lambda b, s: (0, 0)
    vec = pl.BlockSpec((None, 1, D_MODEL), lambda b, s: (b, 0, 0))
    return pl.pallas_call(
        _merge_kernel,
        grid=(B, S // tm),
        in_specs=[
            pl.BlockSpec((None, tm, W_HEADS), row),
            pl.BlockSpec((None, tm, W_HEADS), row),
            pl.BlockSpec((None, tm, D_MODEL), row),
            pl.BlockSpec((None, tm, D_MODEL), row),
            pl.BlockSpec((None, tm, D_MODEL), row),
            vec, vec, vec,
            pl.BlockSpec((1, D_MODEL), const2),
            pl.BlockSpec((W_HEADS, D_MODEL), const2),
            pl.BlockSpec((W_HEADS, D_MODEL), const2),
            pl.BlockSpec((D_MODEL, D_MODEL), const2),
        ],
        out_specs=[pl.BlockSpec((None, tm, D_MODEL), row),
                   pl.BlockSpec((None, tm, D_MODEL), row)],
        out_shape=[jax.ShapeDtypeStruct((B, S, D_MODEL), F32),
                   jax.ShapeDtypeStruct((B, S, D_MODEL), BF16)],
        compiler_params=_params(2),
        name="merge",
    )(oa, ob, ga, gb, x, g1, sc2, sh2, n2, wa, wb, wo)


def _ffn_kernel(h_ref, x_ref, g2_ref, wup_ref, cw_ref, cb_ref, wdn_ref, fg_ref, o_ref,
                carry_scr, *, tm, fc, final):
    s_idx = pl.program_id(1)

    @pl.when(s_idx == 0)
    def _():
        carry_scr[...] = jnp.zeros_like(carry_scr)

    hb = h_ref[...]
    row = lax.broadcasted_iota(I32, (tm, fc), 0)
    acc = jnp.zeros((tm, D_MODEL), F32)
    for cblk in range(D_FF // fc):
        halves = []
        for part in range(2):
            col = part * D_FF + cblk * fc
            u = jnp.dot(hb, wup_ref[:, col:col + fc], preferred_element_type=F32)
            prev = carry_scr[:, col:col + fc]
            u1 = jnp.where(row == 0, prev[7:8], pltpu.roll(u, 1, 0))
            u2 = jnp.where(row == 0, prev[6:7],
                           jnp.where(row == 1, prev[7:8], pltpu.roll(u, 2, 0)))
            carry_scr[:, col:col + fc] = u[tm - SUBLANES:tm]
            cw = cw_ref[:, col:col + fc]
            halves.append(cb_ref[:, col:col + fc] + (cw[0:1] * u2 + cw[1:2] * u1 + cw[2:3] * u))
        act = (jax.nn.silu(halves[0]) * halves[1]).astype(BF16)
        acc = acc + jnp.dot(act, wdn_ref[cblk * fc:(cblk + 1) * fc, :],
                            preferred_element_type=F32)
    x2 = x_ref[...] + g2_ref[...] * acc
    if final:
        var = jnp.mean(x2 * x2, axis=-1, keepdims=True)
        x2 = x2 * lax.rsqrt(var + EPS) * fg_ref[...]
    o_ref[...] = x2


def _ffn_call(h2, x1, g2, wup, cw, cb, wdn, fg, *, tm, fc, final):
    B, S, _ = x1.shape
    row = lambda b, s: (b, s, 0)
    const2 = lambda b, s: (0, 0)
    return pl.pallas_call(
        functools.partial(_ffn_kernel, tm=tm, fc=fc, final=final),
        grid=(B, S // tm),
        in_specs=[
            pl.BlockSpec((None, tm, D_MODEL), row),
            pl.BlockSpec((None, tm, D_MODEL), row),
            pl.BlockSpec((None, 1, D_MODEL), lambda b, s: (b, 0, 0)),
            pl.BlockSpec((D_MODEL, 2 * D_FF), const2, pipeline_mode=pl.Buffered(1)),
            pl.BlockSpec((SUBLANES, 2 * D_FF), const2),
            pl.BlockSpec((1, 2 * D_FF), const2),
            pl.BlockSpec((D_FF, D_MODEL), const2, pipeline_mode=pl.Buffered(1)),
            pl.BlockSpec((1, D_MODEL), const2),
        ],
        out_specs=pl.BlockSpec((None, tm, D_MODEL), row),
        out_shape=jax.ShapeDtypeStruct((B, S, D_MODEL), F32),
        scratch_shapes=[pltpu.VMEM((SUBLANES, 2 * D_FF), F32)],
        compiler_params=_params(2),
        name="ffn",
    )(h2, x1, g2, wup, cw, cb, wdn, fg)


def _reorder_w_in(w):
    o = np.cumsum([0, W_HEADS, HEAD_DIM, HEAD_DIM, W_HEADS, HEAD_DIM, N_HEADS,
                   W_HEADS, W_HEADS, W_HEADS, N_HEADS, 2 * D_MODEL])
    seg = lambda k: w[:, o[k]:o[k + 1]]
    q_a, k_a, v_a, q_i, k_i, w_i, q_f, k_f, v_f, f_f, gates = [seg(k) for k in range(11)]
    pad = jnp.zeros((w.shape[0], N_COLS - int(o[-1])), w.dtype)
    return jnp.concatenate([q_a, q_i, q_f, k_f, v_f, gates, k_a, k_i, v_a, w_i, f_f, pad],
                           axis=1).astype(BF16)


def kernel(x, c, positions, mod_w, mod_b, norm1_g, norm2_g, w_in, forget_bias, w_branch_a,
           w_branch_b, w_out, w_up, conv_w, conv_b, w_down, final_g):
    B, S, _ = x.shape
    depth = mod_w.shape[0]
    tm = min(512, S)
    mod = _mod_call(c, mod_w, mod_b)[:, :, :B].reshape(depth, 6, B, 1, D_MODEL)
    pos = positions.astype(F32).reshape(B, S, 1)

    inv_freq = ROPE_THETA ** (-jnp.arange(0, ROT_DIM, 2, dtype=F32) / ROT_DIM)
    l64 = np.arange(LANES) % HEAD_DIM
    invf = jnp.where(l64 < ROT_DIM, inv_freq[l64 % (ROT_DIM // 2)], 0.0).reshape(1, LANES)
    place = np.zeros((N_HEADS, HEAD_DIM, W_HEADS), np.float32)
    for h in range(N_HEADS):
        place[h, np.arange(HEAD_DIM), h * HEAD_DIM + np.arange(HEAD_DIM)] = 1.0
    place = jnp.asarray(place, BF16)

    for l in range(depth):
        sh1, sc1, g1, sh2, sc2, g2 = [mod[l, k] for k in range(6)]
        fb = jnp.zeros((1, LANES), F32).at[0, LANE_FORGET:LANE_FORGET + N_HEADS].set(forget_bias[l])
        (qa, qi, qf, kf, vf, ga, gb, kat, kit, va, small, cumt) = _in_call(
            x, sh1, sc1, norm1_g[l].reshape(1, D_MODEL), pos, invf, fb, _reorder_w_in(w_in[l]),
            tm=tm)
        oa = _dsa_call(qi, qa, small, kit, kat, va, place, q_blk=128, tk=min(512, S))
        ob = _fox_call(qf, kf, vf, small, cumt, t=tm)
        x1, h2 = _merge_call(oa, ob, ga, gb, x, g1, sc2, sh2, norm2_g[l].reshape(1, D_MODEL),
                             w_branch_a[l].astype(BF16), w_branch_b[l].astype(BF16),
                             w_out[l].astype(BF16), tm=tm)
        cw8 = jnp.zeros((SUBLANES, 2 * D_FF), F32).at[:conv_w.shape[1]].set(conv_w[l])
        x = _ffn_call(h2, x1, g2, w_up[l].astype(BF16), cw8, conv_b[l].reshape(1, 2 * D_FF),
                      w_down[l].astype(BF16), final_g.reshape(1, D_MODEL),
                      tm=tm, fc=256, final=(l == depth - 1))
    return x
```

```python
import functools

import numpy as np
import jax
import jax.numpy as jnp
from jax import lax
from jax.experimental import pallas as pl
from jax.experimental.pallas import tpu as pltpu

F32 = jnp.float32
BF16 = jnp.bfloat16
I32 = jnp.int32

D_MODEL = 1024
CHUNK = 64
HEAD_DIM = 64
N_HEADS = 8
TOPK_MAX = 256
ROPE_THETA = 500000.0
ROT_DIM = HEAD_DIM // 4
D_FF = 2816
EPS = 1e-6
NEG_INF = -1e30
W_HEADS = N_HEADS * HEAD_DIM

LANES = 128
SUBLANES = 8
VMEM_LIMIT = 56 * 1024 * 1024

TOKEN_TILE = 512
DSA_Q_BLOCK = LANES
DSA_KEY_TILE = 512
FFN_CHUNK = 256

COL_QA = 0
COL_QI = 512
COL_QF = 1024
COL_KF = 1536
COL_VF = 2048
COL_GATES = 2560
COL_SMALL = 4608
N_COLS = 4864
LANE_WIDX = 64
LANE_FORGET = 72

INT_MIN = -(2 ** 31)
M_INIT = -3.0e38


def _f32_key(v):
    b = int(np.float32(v).view(np.int32))
    return b ^ ((b >> 31) & 0x7FFFFFFF)


MIN_VALID_KEY = _f32_key(0.5 * NEG_INF) + 1


def _params(n_grid):
    return pltpu.CompilerParams(
        dimension_semantics=("arbitrary",) * n_grid, vmem_limit_bytes=VMEM_LIMIT)


def _col_reduce(x, op):
    rows = x.shape[0]
    part = op(x.reshape(rows // SUBLANES, SUBLANES, x.shape[1]), axis=0)
    return op(part, axis=0, keepdims=True)


def _mod_kernel(c_ref, w_ref, b_ref, o_ref):
    o_ref[...] = jnp.dot(c_ref[...], w_ref[...], precision=lax.Precision.HIGHEST,
                         preferred_element_type=F32) + b_ref[...]


def _mod_call(c, mod_w, mod_b):
    depth = mod_w.shape[0]
    c8 = jnp.zeros((SUBLANES, D_MODEL), F32).at[: c.shape[0]].set(c)
    b4 = mod_b.reshape(depth, 6, 1, D_MODEL)
    return pl.pallas_call(
        _mod_kernel,
        grid=(depth, 6),
        in_specs=[
            pl.BlockSpec((SUBLANES, D_MODEL), lambda l, j: (0, 0)),
            pl.BlockSpec((None, D_MODEL, D_MODEL), lambda l, j: (l, 0, j)),
            pl.BlockSpec((None, None, 1, D_MODEL), lambda l, j: (l, j, 0, 0)),
        ],
        out_specs=pl.BlockSpec((None, None, SUBLANES, D_MODEL), lambda l, j: (l, j, 0, 0)),
        out_shape=jax.ShapeDtypeStruct((depth, 6, SUBLANES, D_MODEL), F32),
        compiler_params=_params(2),
        name="mod",
    )(c8, mod_w, b4)


def _rms_mod(x, gain, scale, shift):
    var = jnp.mean(x * x, axis=-1, keepdims=True)
    y = x * lax.rsqrt(var + EPS) * gain
    return y * (1.0 + scale) + shift


def _in_kernel(x_ref, sh_ref, sc_ref, g_ref, pos_ref, invf_ref, fb_ref, w_ref,
               qat_ref, qit_ref, qft_ref, kf_ref, vft_ref, ga_ref, gb_ref,
               kk_ref, vat_ref, small_ref, auxt_ref, carry_scr, *, tm):
    s_idx = pl.program_id(1)

    @pl.when(s_idx == 0)
    def _():
        carry_scr[...] = jnp.zeros_like(carry_scr)

    h = _rms_mod(x_ref[...], g_ref[...], sc_ref[...], sh_ref[...])
    hb = h.astype(BF16)

    ang = pos_ref[...] * invf_ref[...]
    cos = jnp.cos(ang)
    sin = jnp.sin(ang)
    l64 = lax.broadcasted_iota(I32, (tm, LANES), 1) & (HEAD_DIM - 1)
    half = ROT_DIM // 2
    t_cos = jnp.where(l64 < ROT_DIM, cos, 1.0)
    t_lo = jnp.where(l64 < half, -sin, 0.0)
    t_hi = jnp.where(l64 < half, 0.0, jnp.where(l64 < ROT_DIM, sin, 0.0))

    def rope(xc):
        return (xc * t_cos + pltpu.roll(xc, LANES - half, 1) * t_lo
                + pltpu.roll(xc, half, 1) * t_hi)

    def proj(col, width):
        return jnp.dot(hb, w_ref[:, col:col + width], preferred_element_type=F32)

    for col, out_ref in ((COL_QA, qat_ref), (COL_QI, qit_ref)):
        p = proj(col, W_HEADS)
        for cblk in range(W_HEADS // LANES):
            r = (rope(p[:, cblk * LANES:(cblk + 1) * LANES]) * 0.125).T.astype(BF16)
            out_ref[2 * cblk] = r[:HEAD_DIM]
            out_ref[2 * cblk + 1] = r[HEAD_DIM:]

    for col, out_ref, scale in ((COL_QF, qft_ref, 0.125), (COL_VF, vft_ref, 1.0)):
        p = proj(col, W_HEADS)
        for cblk in range(W_HEADS // LANES):
            sl = slice(cblk * LANES, (cblk + 1) * LANES)
            out_ref[sl, :] = (p[:, sl] * scale).T.astype(BF16)
    kf_ref[...] = proj(COL_KF, W_HEADS).astype(BF16)
    ga_ref[...] = jax.nn.sigmoid(proj(COL_GATES, D_MODEL)).astype(BF16)
    gb_ref[...] = jax.nn.sigmoid(proj(COL_GATES + D_MODEL, D_MODEL)).astype(BF16)

    small = proj(COL_SMALL, 2 * LANES)
    kk_ref[...] = rope(small[:, :LANES]).astype(BF16)
    blk = small[:, LANES:]

    lane = lax.broadcasted_iota(I32, (tm, LANES), 1)
    is_f = (lane >= LANE_FORGET) & (lane < LANE_FORGET + N_HEADS)
    z = blk + fb_ref[...]
    logf = jnp.minimum(z, 0.0) - jnp.log1p(jnp.exp(-jnp.abs(z)))
    logf = jnp.where(is_f, logf, 0.0)
    tri = (lax.broadcasted_iota(I32, (tm, tm), 0)
           >= lax.broadcasted_iota(I32, (tm, tm), 1)).astype(BF16)
    p0 = logf.astype(BF16)
    r1 = logf - p0.astype(F32)
    p1 = r1.astype(BF16)
    p2 = (r1 - p1.astype(F32)).astype(BF16)
    cum = (jnp.dot(tri, p0, preferred_element_type=F32)
           + jnp.dot(tri, p1, preferred_element_type=F32)
           + jnp.dot(tri, p2, preferred_element_type=F32)) + carry_scr[0:1, :]
    carry_scr[0:1, :] = cum[tm - 1:tm, :]

    is_w = (lane >= LANE_WIDX) & (lane < LANE_WIDX + N_HEADS)
    comb = jnp.where(is_w, blk * (N_HEADS ** -0.5), jnp.where(is_f, cum, blk))
    small_ref[...] = comb
    comb_t = comb.T
    vat_ref[...] = comb_t[:HEAD_DIM].astype(BF16)
    auxt_ref[...] = comb_t[LANE_WIDX:LANE_WIDX + 2 * N_HEADS]


def _in_call(x, sh, sc, gain, pos, invf, fb, w_all, *, tm):
    B, S, _ = x.shape
    hm_t = jax.ShapeDtypeStruct((B, N_HEADS, HEAD_DIM, S), BF16)
    wide = jax.ShapeDtypeStruct((B, S, W_HEADS), BF16)
    wide_t = jax.ShapeDtypeStruct((B, W_HEADS, S), BF16)
    gate = jax.ShapeDtypeStruct((B, S, D_MODEL), BF16)
    row = lambda b, s: (b, s, 0)
    col = lambda b, s: (b, 0, s)
    vec = pl.BlockSpec((None, 1, D_MODEL), lambda b, s: (b, 0, 0))
    const2 = lambda b, s: (0, 0)
    hm_spec = pl.BlockSpec((None, N_HEADS, HEAD_DIM, tm), lambda b, s: (b, 0, 0, s))
    return pl.pallas_call(
        functools.partial(_in_kernel, tm=tm),
        grid=(B, S // tm),
        in_specs=[
            pl.BlockSpec((None, tm, D_MODEL), row),
            vec, vec,
            pl.BlockSpec((1, D_MODEL), const2),
            pl.BlockSpec((None, tm, 1), row),
            pl.BlockSpec((1, LANES), const2),
            pl.BlockSpec((1, LANES), const2),
            pl.BlockSpec((D_MODEL, N_COLS), const2),
        ],
        out_specs=[
            hm_spec, hm_spec,
            pl.BlockSpec((None, W_HEADS, tm), col),
            pl.BlockSpec((None, tm, W_HEADS), row),
            pl.BlockSpec((None, W_HEADS, tm), col),
            pl.BlockSpec((None, tm, D_MODEL), row),
            pl.BlockSpec((None, tm, D_MODEL), row),
            pl.BlockSpec((None, tm, LANES), row),
            pl.BlockSpec((None, HEAD_DIM, tm), col),
            pl.BlockSpec((None, tm, LANES), row),
            pl.BlockSpec((None, 2 * N_HEADS, tm), col),
        ],
        out_shape=[hm_t, hm_t, wide_t, wide, wide_t, gate, gate,
                   jax.ShapeDtypeStruct((B, S, LANES), BF16),
                   jax.ShapeDtypeStruct((B, HEAD_DIM, S), BF16),
                   jax.ShapeDtypeStruct((B, S, LANES), F32),
                   jax.ShapeDtypeStruct((B, 2 * N_HEADS, S), F32)],
        scratch_shapes=[pltpu.VMEM((SUBLANES, LANES), F32)],
        compiler_params=_params(2),
        name="in_proj",
    )(x, sh, sc, gain, pos, invf, fb, w_all)


def _dsa_kernel(qit_ref, qat_ref, aux_ref, kk_ref, vat_ref, o_ref,
                keys_scr, acc_scr, p_scr, *, q_blk, tk, topk, idx_bits):
    i = pl.program_id(1)
    width = N_HEADS * q_blk
    n_tiles = ((i + 1) * q_blk + tk - 1) // tk

    zeros = jnp.zeros((HEAD_DIM, width), BF16)
    qi_ext = jnp.concatenate(
        [jnp.concatenate([qit_ref[h] for h in range(N_HEADS)], axis=1), zeros], axis=0)
    qa_ext = jnp.concatenate(
        [zeros, jnp.concatenate([qat_ref[h] for h in range(N_HEADS)], axis=1)], axis=0)
    w_rows = aux_ref[0:N_HEADS, :]
    q_pos = i * q_blk + lax.broadcasted_iota(I32, (1, q_blk), 1)
    key_lim = (q_pos // CHUNK + 1) * CHUNK
    row_iota = lax.broadcasted_iota(I32, (tk, q_blk), 0)

    def score_tile(t, _):
        off = pl.multiple_of(t * tk, tk)
        s = jnp.dot(kk_ref[pl.ds(off, tk), :], qi_ext, preferred_element_type=F32)
        acc = jnp.zeros((tk, q_blk), F32)
        for h in range(N_HEADS):
            acc = acc + w_rows[h:h + 1, :] * jnp.maximum(s[:, h * q_blk:(h + 1) * q_blk], 0.0)
        bits = pltpu.bitcast(acc, I32)
        key = bits ^ ((bits >> 31) & 0x7FFFFFFF)
        keys_scr[pl.ds(off, tk), :] = jnp.where(off + row_iota < key_lim, key, INT_MIN)
        return 0

    lax.fori_loop(0, n_tiles, score_tile, 0)

    def count(pred):
        def tile(t, acc):
            off = pl.multiple_of(t * tk, tk)
            x = pred(keys_scr[pl.ds(off, tk), :], off + row_iota)
            return acc + jnp.sum(x.reshape(tk // 64, 64, q_blk), axis=0)
        acc = lax.fori_loop(0, n_tiles, tile, jnp.zeros((64, q_blk), I32))
        return _col_reduce(acc, jnp.sum)

    def radix_pass(b, carry):
        prefix, cnt_ge = carry
        cand = prefix | jnp.left_shift(jnp.int32(1), 31 - b)
        cand_s = cand ^ INT_MIN
        cnt = count(lambda k, _: jnp.where(k >= cand_s, 1, 0))
        ok = cnt >= topk
        return jnp.where(ok, cand, prefix), jnp.where(ok, cnt, cnt_ge)

    prefix0 = jnp.zeros((1, q_blk), I32)
    total = jnp.zeros((1, q_blk), I32) + n_tiles * tk
    prefix, cnt_ge = lax.fori_loop(0, 32, radix_pass, (prefix0, total))
    kth = prefix ^ INT_MIN
    thr = jnp.maximum(kth, MIN_VALID_KEY)

    has_tie = jnp.where(cnt_ge > topk, jnp.where(kth >= MIN_VALID_KEY, 1, 0), 0)

    @pl.when(jnp.max(has_tie) > 0)
    def _():
        n_gt = count(lambda k, _: jnp.where(k > thr, 1, 0))
        need = topk - n_gt

        def idx_pass(b, pre):
            cand = pre | jnp.left_shift(jnp.int32(1), idx_bits - 1 - b)
            c = count(lambda k, kidx: jnp.where(k == thr, jnp.where(kidx < cand, 1, 0), 0))
            return jnp.where(c < need, cand, pre)

        last = lax.fori_loop(0, idx_bits, idx_pass, jnp.zeros((1, q_blk), I32))

        def drop_tile(t, _):
            off = pl.multiple_of(t * tk, tk)
            k = keys_scr[pl.ds(off, tk), :]
            drop = jnp.where(k == thr, jnp.where(off + row_iota > last, INT_MIN, k), k)
            keys_scr[pl.ds(off, tk), :] = drop
            return 0

        lax.fori_loop(0, n_tiles, drop_tile, 0)

    acc_scr[...] = jnp.zeros_like(acc_scr)

    def attn_tile(t, carry):
        m_all, l_all = carry
        off = pl.multiple_of(t * tk, tk)
        logits = jnp.dot(kk_ref[pl.ds(off, tk), :], qa_ext, preferred_element_type=F32)
        sel = keys_scr[pl.ds(off, tk), :] >= thr
        m_rows, l_rows, alphas = [], [], []
        for h in range(N_HEADS):
            cols = slice(h * q_blk, (h + 1) * q_blk)
            s = jnp.where(sel, logits[:, cols], NEG_INF)
            m_prev = m_all[h:h + 1, :]
            m_new = jnp.maximum(m_prev, _col_reduce(s, jnp.max))
            p = jnp.exp(s - m_new)
            alpha = jnp.exp(m_prev - m_new)
            l_rows.append(alpha * l_all[h:h + 1, :] + _col_reduce(p, jnp.sum))
            m_rows.append(m_new)
            alphas.append(alpha)
            p_scr[:, cols] = p.astype(BF16)
        pv = jnp.dot(vat_ref[:, pl.ds(off, tk)], p_scr[...], preferred_element_type=F32)
        acc_scr[...] = acc_scr[...] * jnp.concatenate(alphas, axis=1) + pv
        return jnp.concatenate(m_rows, axis=0), jnp.concatenate(l_rows, axis=0)

    m0 = jnp.full((N_HEADS, q_blk), M_INIT, F32)
    l0 = jnp.zeros((N_HEADS, q_blk), F32)
    _, l_all = lax.fori_loop(0, n_tiles, attn_tile, (m0, l0))

    for pair in range(N_HEADS // 2):
        halves = []
        for h in (2 * pair, 2 * pair + 1):
            halves.append(acc_scr[:, h * q_blk:(h + 1) * q_blk] / l_all[h:h + 1, :])
        o_ref[:, pair * LANES:(pair + 1) * LANES] = (
            jnp.concatenate(halves, axis=0).T.astype(BF16))


def _dsa_call(qit, qat, auxt, kk, vat, *, q_blk, tk):
    B, _, _, S = qit.shape
    topk = min(TOPK_MAX, S // 4)
    idx_bits = max(1, int(np.ceil(np.log2(S))))
    width = N_HEADS * q_blk
    hm_spec = pl.BlockSpec((None, N_HEADS, HEAD_DIM, q_blk), lambda b, i: (b, 0, 0, i))
    return pl.pallas_call(
        functools.partial(_dsa_kernel, q_blk=q_blk, tk=tk, topk=topk, idx_bits=idx_bits),
        grid=(B, S // q_blk),
        in_specs=[
            hm_spec, hm_spec,
            pl.BlockSpec((None, 2 * N_HEADS, q_blk), lambda b, i: (b, 0, i)),
            pl.BlockSpec((None, S, LANES), lambda b, i: (b, 0, 0)),
            pl.BlockSpec((None, HEAD_DIM, S), lambda b, i: (b, 0, 0)),
        ],
        out_specs=pl.BlockSpec((None, q_blk, W_HEADS), lambda b, i: (b, i, 0)),
        out_shape=jax.ShapeDtypeStruct((B, S, W_HEADS), BF16),
        scratch_shapes=[
            pltpu.VMEM((S, q_blk), I32),
            pltpu.VMEM((HEAD_DIM, width), F32),
            pltpu.VMEM((tk, width), BF16),
        ],
        compiler_params=_params(2),
        name="dsa",
    )(qit, qat, auxt, kk, vat)


def _fox_kernel(qi_tbl, kj_tbl, qt_ref, k_ref, vt_ref, auxq_ref, smallk_ref, o_ref,
                m_scr, l_scr, acc_scr, *, t):
    step = pl.program_id(1)
    i = qi_tbl[step]
    j = kj_tbl[step]
    top = lax.broadcasted_iota(I32, (LANES, t), 0) < HEAD_DIM

    @pl.when(j == 0)
    def _():
        m_scr[...] = jnp.full_like(m_scr, M_INIT)
        l_scr[...] = jnp.zeros_like(l_scr)
        acc_scr[...] = jnp.zeros_like(acc_scr)

    def body(diag):
        if diag:
            causal = (lax.broadcasted_iota(I32, (t, t), 0)
                      <= lax.broadcasted_iota(I32, (t, t), 1))
        for pair in range(N_HEADS // 2):
            rows = slice(pair * LANES, (pair + 1) * LANES)
            qp = qt_ref[rows, :]
            kp = k_ref[:, rows]
            vp = vt_ref[rows, :]
            alphas, pvs = [], []
            for e in range(2):
                h = 2 * pair + e
                qm = jnp.where(top, qp, 0) if e == 0 else jnp.where(top, 0, qp)
                s = jnp.dot(kp, qm, preferred_element_type=F32)
                s = (s + auxq_ref[N_HEADS + h:N_HEADS + h + 1, :]
                     - smallk_ref[:, LANE_FORGET + h:LANE_FORGET + h + 1])
                if diag:
                    s = jnp.where(causal, s, NEG_INF)
                m_prev = m_scr[h:h + 1, :]
                m_new = jnp.maximum(m_prev, _col_reduce(s, jnp.max))
                p = jnp.exp(s - m_new)
                alpha = jnp.exp(m_prev - m_new)
                l_scr[h:h + 1, :] = alpha * l_scr[h:h + 1, :] + _col_reduce(p, jnp.sum)
                m_scr[h:h + 1, :] = m_new
                alphas.append(alpha)
                pvs.append(jnp.dot(vp, p.astype(BF16), preferred_element_type=F32))
            acc_scr[pair] = (acc_scr[pair] * jnp.where(top, alphas[0], alphas[1])
                             + jnp.where(top, pvs[0], pvs[1]))

    @pl.when(j < i)
    def _():
        body(False)

    @pl.when(j == i)
    def _():
        body(True)
        for pair in range(N_HEADS // 2):
            denom = jnp.where(top, l_scr[2 * pair:2 * pair + 1, :],
                              l_scr[2 * pair + 1:2 * pair + 2, :])
            o_ref[:, pair * LANES:(pair + 1) * LANES] = (acc_scr[pair] / denom).T.astype(BF16)


def _fox_call(qt, k, vt, auxt, small, *, t):
    B, S, _ = k.shape
    n = S // t
    qi_tbl = np.concatenate([np.full(i + 1, i, np.int32) for i in range(n)])
    kj_tbl = np.concatenate([np.arange(i + 1, dtype=np.int32) for i in range(n)])
    grid_spec = pltpu.PrefetchScalarGridSpec(
        num_scalar_prefetch=2,
        grid=(B, len(qi_tbl)),
        in_specs=[
            pl.BlockSpec((None, W_HEADS, t), lambda b, s, qi, kj: (b, 0, qi[s])),
            pl.BlockSpec((None, t, W_HEADS), lambda b, s, qi, kj: (b, kj[s], 0)),
            pl.BlockSpec((None, W_HEADS, t), lambda b, s, qi, kj: (b, 0, kj[s])),
            pl.BlockSpec((None, 2 * N_HEADS, t), lambda b, s, qi, kj: (b, 0, qi[s])),
            pl.BlockSpec((None, t, LANES), lambda b, s, qi, kj: (b, kj[s], 0)),
        ],
        out_specs=pl.BlockSpec((None, t, W_HEADS), lambda b, s, qi, kj: (b, qi[s], 0)),
        scratch_shapes=[
            pltpu.VMEM((N_HEADS, t), F32),
            pltpu.VMEM((N_HEADS, t), F32),
            pltpu.VMEM((N_HEADS // 2, LANES, t), F32),
        ],
    )
    return pl.pallas_call(
        functools.partial(_fox_kernel, t=t),
        grid_spec=grid_spec,
        out_shape=jax.ShapeDtypeStruct((B, S, W_HEADS), BF16),
        compiler_params=_params(2),
        name="fox",
    )(jnp.asarray(qi_tbl), jnp.asarray(kj_tbl), qt, k, vt, auxt, small)


def _merge_kernel(oa_ref, ob_ref, ga_ref, gb_ref, x_ref, g1_ref, sc_ref, sh_ref, n2_ref,
                  wa_ref, wb_ref, wo_ref, x1_ref, h2_ref):
    ya = jnp.dot(oa_ref[...], wa_ref[...], preferred_element_type=F32)
    yb = jnp.dot(ob_ref[...], wb_ref[...], preferred_element_type=F32)
    mix = ga_ref[...].astype(F32) * ya + gb_ref[...].astype(F32) * yb
    y = jnp.dot(mix.astype(BF16), wo_ref[...], preferred_element_type=F32)
    x1 = x_ref[...] + g1_ref[...] * y
    x1_ref[...] = x1
    h2_ref[...] = _rms_mod(x1, n2_ref[...], sc_ref[...], sh_ref[...]).astype(BF16)


def _merge_call(oa, ob, ga, gb, x, g1, sc2, sh2, n2, wa, wb, wo, *, tm):
    B, S, _ = x.shape
    row = lambda b, s: (b, s, 0)
    const2 = lambda b, s: (0, 0)
    vec = pl.BlockSpec((None, 1, D_MODEL), lambda b, s: (b, 0, 0))
    return pl.pallas_call(
        _merge_kernel,
        grid=(B, S // tm),
        in_specs=[
            pl.BlockSpec((None, tm, W_HEADS), row),
            pl.BlockSpec((None, tm, W_HEADS), row),
            pl.BlockSpec((None, tm, D_MODEL), row),
            pl.BlockSpec((None, tm, D_MODEL), row),
            pl.BlockSpec((None, tm, D_MODEL), row),
            vec, vec, vec,
            pl.BlockSpec((1, D_MODEL), const2),
            pl.BlockSpec((W_HEADS, D_MODEL), const2),
            pl.BlockSpec((W_HEADS, D_MODEL), const2),
            pl.BlockSpec((D_MODEL, D_MODEL), const2),
        ],
        out_specs=[pl.BlockSpec((None, tm, D_MODEL), row),
                   pl.BlockSpec((None, tm, D_MODEL), row)],
        out_shape=[jax.ShapeDtypeStruct((B, S, D_MODEL), F32),
                   jax.ShapeDtypeStruct((B, S, D_MODEL), BF16)],
        compiler_params=_params(2),
        name="merge",
    )(oa, ob, ga, gb, x, g1, sc2, sh2, n2, wa, wb, wo)


def _ffn_kernel(h_ref, x_ref, g2_ref, wup_ref, cw_ref, cb_ref, wdn_ref, fg_ref, o_ref,
                carry_scr, *, tm, fc, final):
    s_idx = pl.program_id(1)

    @pl.when(s_idx == 0)
    def _():
        carry_scr[...] = jnp.zeros_like(carry_scr)

    hb = h_ref[...]
    row = lax.broadcasted_iota(I32, (tm, fc), 0)
    acc = jnp.zeros((tm, D_MODEL), F32)
    for cblk in range(D_FF // fc):
        halves = []
        for part in range(2):
            col = part * D_FF + cblk * fc
            u = jnp.dot(hb, wup_ref[:, col:col + fc], preferred_element_type=F32)
            prev = carry_scr[:, col:col + fc]
            u1 = jnp.where(row == 0, prev[7:8], pltpu.roll(u, 1, 0))
            u2 = jnp.where(row == 0, prev[6:7],
                           jnp.where(row == 1, prev[7:8], pltpu.roll(u, 2, 0)))
            carry_scr[:, col:col + fc] = u[tm - SUBLANES:tm]
            cw = cw_ref[:, col:col + fc]
            halves.append(cb_ref[:, col:col + fc] + (cw[0:1] * u2 + cw[1:2] * u1 + cw[2:3] * u))
        act = (jax.nn.silu(halves[0]) * halves[1]).astype(BF16)
        acc = acc + jnp.dot(act, wdn_ref[cblk * fc:(cblk + 1) * fc, :],
                            preferred_element_type=F32)
    x2 = x_ref[...] + g2_ref[...] * acc
    if final:
        var = jnp.mean(x2 * x2, axis=-1, keepdims=True)
        x2 = x2 * lax.rsqrt(var + EPS) * fg_ref[...]
    o_ref[...] = x2


def _ffn_call(h2, x1, g2, wup, cw, cb, wdn, fg, *, tm, fc, final):
    B, S, _ = x1.shape
    row = lambda b, s: (b, s, 0)
    const2 = lambda b, s: (0, 0)
    return pl.pallas_call(
        functools.partial(_ffn_kernel, tm=tm, fc=fc, final=final),
        grid=(B, S // tm),
        in_specs=[
            pl.BlockSpec((None, tm, D_MODEL), row),
            pl.BlockSpec((None, tm, D_MODEL), row),
            pl.BlockSpec((None, 1, D_MODEL), lambda b, s: (b, 0, 0)),
            pl.BlockSpec((D_MODEL, 2 * D_FF), const2, pipeline_mode=pl.Buffered(1)),
            pl.BlockSpec((SUBLANES, 2 * D_FF), const2),
            pl.BlockSpec((1, 2 * D_FF), const2),
            pl.BlockSpec((D_FF, D_MODEL), const2, pipeline_mode=pl.Buffered(1)),
            pl.BlockSpec((1, D_MODEL), const2),
        ],
        out_specs=pl.BlockSpec((None, tm, D_MODEL), row),
        out_shape=jax.ShapeDtypeStruct((B, S, D_MODEL), F32),
        scratch_shapes=[pltpu.VMEM((SUBLANES, 2 * D_FF), F32)],
        compiler_params=_params(2),
        name="ffn",
    )(h2, x1, g2, wup, cw, cb, wdn, fg)


def _reorder_w_in(w):
    o = np.cumsum([0, W_HEADS, HEAD_DIM, HEAD_DIM, W_HEADS, HEAD_DIM, N_HEADS,
                   W_HEADS, W_HEADS, W_HEADS, N_HEADS, 2 * D_MODEL])
    seg = lambda k: w[:, o[k]:o[k + 1]]
    q_a, k_a, v_a, q_i, k_i, w_i, q_f, k_f, v_f, f_f, gates = [seg(k) for k in range(11)]
    pad = jnp.zeros((w.shape[0], N_COLS - int(o[-1])), w.dtype)
    return jnp.concatenate([q_a, q_i, q_f, k_f, v_f, gates, k_i, k_a, v_a, w_i, f_f, pad],
                           axis=1).astype(BF16)


def kernel(x, c, positions, mod_w, mod_b, norm1_g, norm2_g, w_in, forget_bias, w_branch_a,
           w_branch_b, w_out, w_up, conv_w, conv_b, w_down, final_g):
    B, S, _ = x.shape
    depth = mod_w.shape[0]
    tm = min(TOKEN_TILE, S)
    mod = _mod_call(c, mod_w, mod_b)[:, :, :B].reshape(depth, 6, B, 1, D_MODEL)
    pos = positions.astype(F32).reshape(B, S, 1)

    inv_freq = ROPE_THETA ** (-jnp.arange(0, ROT_DIM, 2, dtype=F32) / ROT_DIM)
    l64 = np.arange(LANES) % HEAD_DIM
    invf = jnp.where(l64 < ROT_DIM, inv_freq[l64 % (ROT_DIM // 2)], 0.0).reshape(1, LANES)

    for l in range(depth):
        sh1, sc1, g1, sh2, sc2, g2 = [mod[l, k] for k in range(6)]
        fb = jnp.zeros((1, LANES), F32).at[0, LANE_FORGET:LANE_FORGET + N_HEADS].set(forget_bias[l])
        (qat, qit, qft, kf, vft, ga, gb, kk, vat, small, auxt) = _in_call(
            x, sh1, sc1, norm1_g[l].reshape(1, D_MODEL), pos, invf, fb, _reorder_w_in(w_in[l]),
            tm=tm)
        oa = _dsa_call(qit, qat, auxt, kk, vat, q_blk=DSA_Q_BLOCK, tk=min(DSA_KEY_TILE, S))
        ob = _fox_call(qft, kf, vft, auxt, small, t=tm)
        x1, h2 = _merge_call(oa, ob, ga, gb, x, g1, sc2, sh2, norm2_g[l].reshape(1, D_MODEL),
                             w_branch_a[l].astype(BF16), w_branch_b[l].astype(BF16),
                             w_out[l].astype(BF16), tm=tm)
        cw8 = jnp.zeros((SUBLANES, 2 * D_FF), F32).at[:conv_w.shape[1]].set(conv_w[l])
        x = _ffn_call(h2, x1, g2, w_up[l].astype(BF16), cw8, conv_b[l].reshape(1, 2 * D_FF),
                      w_down[l].astype(BF16), final_g.reshape(1, D_MODEL),
                      tm=tm, fc=FFN_CHUNK, final=(l == depth - 1))
    return x
```

```python
import functools

import numpy as np
import jax
import jax.numpy as jnp
from jax import lax
from jax.experimental import pallas as pl
from jax.experimental.pallas import tpu as pltpu

F32 = jnp.float32
BF16 = jnp.bfloat16
I32 = jnp.int32

D_MODEL = 1024
CHUNK = 64
HEAD_DIM = 64
N_HEADS = 8
TOPK_MAX = 256
ROPE_THETA = 500000.0
ROT_DIM = HEAD_DIM // 4
D_FF = 2816
EPS = 1e-6
NEG_INF = -1e30
W_HEADS = N_HEADS * HEAD_DIM

LANES = 128
SUBLANES = 8
VMEM_LIMIT = 56 * 1024 * 1024

TOKEN_TILE = 512
DSA_Q_BLOCK = LANES
DSA_KEY_TILE = 512
FFN_CHUNK = 256

COL_QA = 0
COL_QI = 512
COL_QF = 1024
COL_KF = 1536
COL_VF = 2048
COL_GATES = 2560
COL_SMALL = 4608
N_COLS = 4864
LANE_WIDX = 64
LANE_FORGET = 72
LANE_ONES = 80
N_PIECES = 3
PIECE_ROWS = 16
LANE_CK = PIECE_ROWS

INT_MIN = -(2 ** 31)
M_INIT = -3.0e38
MAX_EXP_ARG = 60.0


def _f32_key(v):
    b = int(np.float32(v).view(np.int32))
    return b ^ ((b >> 31) & 0x7FFFFFFF)


MIN_VALID_KEY = _f32_key(0.5 * NEG_INF) + 1


def _params(n_grid):
    return pltpu.CompilerParams(
        dimension_semantics=("arbitrary",) * n_grid, vmem_limit_bytes=VMEM_LIMIT)


def _col_reduce(x, op):
    rows = x.shape[0]
    part = op(x.reshape(rows // SUBLANES, SUBLANES, x.shape[1]), axis=0)
    return op(part, axis=0, keepdims=True)


def _mod_kernel(c_ref, w_ref, b_ref, o_ref):
    o_ref[...] = jnp.dot(c_ref[...], w_ref[...], precision=lax.Precision.HIGHEST,
                         preferred_element_type=F32) + b_ref[...]


def _mod_call(c, mod_w, mod_b):
    depth = mod_w.shape[0]
    c8 = jnp.zeros((SUBLANES, D_MODEL), F32).at[: c.shape[0]].set(c)
    b4 = mod_b.reshape(depth, 6, 1, D_MODEL)
    return pl.pallas_call(
        _mod_kernel,
        grid=(depth, 6),
        in_specs=[
            pl.BlockSpec((SUBLANES, D_MODEL), lambda l, j: (0, 0)),
            pl.BlockSpec((None, D_MODEL, D_MODEL), lambda l, j: (l, 0, j)),
            pl.BlockSpec((None, None, 1, D_MODEL), lambda l, j: (l, j, 0, 0)),
        ],
        out_specs=pl.BlockSpec((None, None, SUBLANES, D_MODEL), lambda l, j: (l, j, 0, 0)),
        out_shape=jax.ShapeDtypeStruct((depth, 6, SUBLANES, D_MODEL), F32),
        compiler_params=_params(2),
        name="mod",
    )(c8, mod_w, b4)


def _rms_mod(x, gain, scale, shift):
    var = jnp.mean(x * x, axis=-1, keepdims=True)
    y = x * lax.rsqrt(var + EPS) * gain
    return y * (1.0 + scale) + shift


def _split3(x):
    p0 = x.astype(BF16)
    r1 = x - p0.astype(F32)
    p1 = r1.astype(BF16)
    return p0, p1, (r1 - p1.astype(F32)).astype(BF16)


def _in_kernel(x_ref, sh_ref, sc_ref, g_ref, pos_ref, invf_ref, fb_ref, place_ref, w_ref,
               qat_ref, qit_ref, qft_ref, kf_ref, vft_ref, ga_ref, gb_ref,
               kk_ref, vat_ref, auxt_ref, carry_scr, *, tm):
    s_idx = pl.program_id(1)

    @pl.when(s_idx == 0)
    def _():
        carry_scr[...] = jnp.zeros_like(carry_scr)

    h = _rms_mod(x_ref[...], g_ref[...], sc_ref[...], sh_ref[...])
    hb = h.astype(BF16)

    ang = pos_ref[...] * invf_ref[...]
    cos = jnp.cos(ang)
    sin = jnp.sin(ang)
    l64 = lax.broadcasted_iota(I32, (tm, LANES), 1) & (HEAD_DIM - 1)
    half = ROT_DIM // 2
    t_cos = jnp.where(l64 < ROT_DIM, cos, 1.0)
    t_lo = jnp.where(l64 < half, -sin, 0.0)
    t_hi = jnp.where(l64 < half, 0.0, jnp.where(l64 < ROT_DIM, sin, 0.0))

    def rope(xc):
        return (xc * t_cos + pltpu.roll(xc, LANES - half, 1) * t_lo
                + pltpu.roll(xc, half, 1) * t_hi)

    def proj(col, width):
        return jnp.dot(hb, w_ref[:, col:col + width], preferred_element_type=F32)

    for col, out_ref in ((COL_QA, qat_ref), (COL_QI, qit_ref)):
        p = proj(col, W_HEADS)
        for cblk in range(W_HEADS // LANES):
            r = (rope(p[:, cblk * LANES:(cblk + 1) * LANES]) * 0.125).T.astype(BF16)
            out_ref[2 * cblk] = r[:HEAD_DIM]
            out_ref[2 * cblk + 1] = r[HEAD_DIM:]

    for col, out_ref, scale in ((COL_QF, qft_ref, 0.125), (COL_VF, vft_ref, 1.0)):
        p = proj(col, W_HEADS)
        for cblk in range(W_HEADS // LANES):
            sl = slice(cblk * LANES, (cblk + 1) * LANES)
            out_ref[sl, :] = (p[:, sl] * scale).T.astype(BF16)
    ga_ref[...] = jax.nn.sigmoid(proj(COL_GATES, D_MODEL)).astype(BF16)
    gb_ref[...] = jax.nn.sigmoid(proj(COL_GATES + D_MODEL, D_MODEL)).astype(BF16)

    lane = lax.broadcasted_iota(I32, (tm, LANES), 1)
    ones_lanes = jnp.where(lane < N_PIECES, 1.0, 0.0)
    small = proj(COL_SMALL, 2 * LANES)
    kk_ref[...] = jnp.concatenate([rope(small[:, :LANES]), ones_lanes], axis=1).astype(BF16)
    blk = small[:, LANES:]

    is_f = (lane >= LANE_FORGET) & (lane < LANE_FORGET + N_HEADS)
    z = blk + fb_ref[...]
    logf = jnp.minimum(z, 0.0) - jnp.log1p(jnp.exp(-jnp.abs(z)))
    logf = jnp.where(is_f, logf, 0.0)
    tri = (lax.broadcasted_iota(I32, (tm, tm), 0)
           >= lax.broadcasted_iota(I32, (tm, tm), 1)).astype(BF16)
    p0, p1, p2 = _split3(logf)
    cum = (jnp.dot(tri, p0, preferred_element_type=F32)
           + jnp.dot(tri, p1, preferred_element_type=F32)
           + jnp.dot(tri, p2, preferred_element_type=F32)) + carry_scr[0:1, :]
    carry_scr[0:1, :] = cum[tm - 1:tm, :]

    c0, c1, c2 = _split3(cum)
    kaux = (jnp.dot(c0, place_ref[0], preferred_element_type=F32)
            + jnp.dot(c1, place_ref[1], preferred_element_type=F32)
            + jnp.dot(c2, place_ref[2], preferred_element_type=F32) + ones_lanes).astype(BF16)
    kf = proj(COL_KF, W_HEADS).astype(BF16)
    for pair in range(N_HEADS // 2):
        kf_ref[:, 2 * pair * LANES:(2 * pair + 1) * LANES] = kf[:, pair * LANES:(pair + 1) * LANES]
        kf_ref[:, (2 * pair + 1) * LANES:(2 * pair + 2) * LANES] = kaux

    is_w = (lane >= LANE_WIDX) & (lane < LANE_WIDX + N_HEADS)
    comb = jnp.where(is_w, blk * (N_HEADS ** -0.5), jnp.where(is_f, cum, blk))
    comb_t = jnp.where(lane == LANE_ONES, 1.0, comb).T
    vat_ref[...] = comb_t.astype(BF16)
    auxt_ref[...] = comb_t[LANE_WIDX:LANE_WIDX + 2 * N_HEADS]


def _in_call(x, sh, sc, gain, pos, invf, fb, place, w_all, *, tm):
    B, S, _ = x.shape
    hm_t = jax.ShapeDtypeStruct((B, N_HEADS, HEAD_DIM, S), BF16)
    wide_t = jax.ShapeDtypeStruct((B, W_HEADS, S), BF16)
    gate = jax.ShapeDtypeStruct((B, S, D_MODEL), BF16)
    row = lambda b, s: (b, s, 0)
    col = lambda b, s: (b, 0, s)
    vec = pl.BlockSpec((None, 1, D_MODEL), lambda b, s: (b, 0, 0))
    const2 = lambda b, s: (0, 0)
    hm_spec = pl.BlockSpec((None, N_HEADS, HEAD_DIM, tm), lambda b, s: (b, 0, 0, s))
    return pl.pallas_call(
        functools.partial(_in_kernel, tm=tm),
        grid=(B, S // tm),
        in_specs=[
            pl.BlockSpec((None, tm, D_MODEL), row),
            vec, vec,
            pl.BlockSpec((1, D_MODEL), const2),
            pl.BlockSpec((None, tm, 1), row),
            pl.BlockSpec((1, LANES), const2),
            pl.BlockSpec((1, LANES), const2),
            pl.BlockSpec((N_PIECES, LANES, LANES), lambda b, s: (0, 0, 0)),
            pl.BlockSpec((D_MODEL, N_COLS), const2),
        ],
        out_specs=[
            hm_spec, hm_spec,
            pl.BlockSpec((None, W_HEADS, tm), col),
            pl.BlockSpec((None, tm, 2 * W_HEADS), row),
            pl.BlockSpec((None, W_HEADS, tm), col),
            pl.BlockSpec((None, tm, D_MODEL), row),
            pl.BlockSpec((None, tm, D_MODEL), row),
            pl.BlockSpec((None, tm, 2 * LANES), row),
            pl.BlockSpec((None, LANES, tm), col),
            pl.BlockSpec((None, 2 * N_HEADS, tm), col),
        ],
        out_shape=[hm_t, hm_t, wide_t,
                   jax.ShapeDtypeStruct((B, S, 2 * W_HEADS), BF16),
                   wide_t, gate, gate,
                   jax.ShapeDtypeStruct((B, S, 2 * LANES), BF16),
                   jax.ShapeDtypeStruct((B, LANES, S), BF16),
                   jax.ShapeDtypeStruct((B, 2 * N_HEADS, S), F32)],
        scratch_shapes=[pltpu.VMEM((SUBLANES, LANES), F32)],
        compiler_params=_params(2),
        name="in_proj",
    )(x, sh, sc, gain, pos, invf, fb, place, w_all)


def _dsa_kernel(qit_ref, qat_ref, aux_ref, kk_ref, vat_ref, o_ref,
                keys_scr, acc_scr, p_scr, rhs_scr, *, q_blk, tk, topk, idx_bits):
    i = pl.program_id(1)
    width = N_HEADS * q_blk
    n_tiles = ((i + 1) * q_blk + tk - 1) // tk

    zeros = jnp.zeros((HEAD_DIM, width), BF16)
    qi_ext = jnp.concatenate(
        [jnp.concatenate([qit_ref[h] for h in range(N_HEADS)], axis=1), zeros], axis=0)
    rhs_scr[...] = jnp.concatenate(
        [zeros, jnp.concatenate([qat_ref[h] for h in range(N_HEADS)], axis=1),
         jnp.zeros((LANES, width), BF16)], axis=0)
    piece_row = lax.broadcasted_iota(I32, (PIECE_ROWS, width), 0)

    def set_row_term(r):
        hi = r.astype(BF16).astype(F32)
        mid = (r - hi).astype(BF16).astype(F32)
        lo = r - hi - mid
        blk = jnp.where(piece_row == 0, hi, jnp.where(piece_row == 1, mid,
                                                      jnp.where(piece_row == 2, lo, 0.0)))
        rhs_scr[2 * HEAD_DIM:2 * HEAD_DIM + PIECE_ROWS, :] = blk.astype(BF16)
    w_rows = aux_ref[0:N_HEADS, :]
    q_pos = i * q_blk + lax.broadcasted_iota(I32, (1, q_blk), 1)
    key_lim = (q_pos // CHUNK + 1) * CHUNK
    row_iota = lax.broadcasted_iota(I32, (tk, q_blk), 0)

    def score_tile(t, _):
        off = pl.multiple_of(t * tk, tk)
        s = jnp.dot(kk_ref[pl.ds(off, tk), 0:LANES], qi_ext,
                    preferred_element_type=F32)
        acc = jnp.zeros((tk, q_blk), F32)
        for h in range(N_HEADS):
            acc = acc + w_rows[h:h + 1, :] * jnp.maximum(s[:, h * q_blk:(h + 1) * q_blk], 0.0)
        bits = pltpu.bitcast(acc, I32)
        key = bits ^ ((bits >> 31) & 0x7FFFFFFF)
        keys_scr[pl.ds(off, tk), :] = jnp.where(off + row_iota < key_lim, key, INT_MIN)
        return 0

    lax.fori_loop(0, n_tiles, score_tile, 0)

    def count(pred):
        def tile(t, acc):
            off = pl.multiple_of(t * tk, tk)
            x = pred(keys_scr[pl.ds(off, tk), :], off + row_iota)
            return acc + jnp.sum(x.reshape(tk // 64, 64, q_blk), axis=0)
        acc = lax.fori_loop(0, n_tiles, tile, jnp.zeros((64, q_blk), I32))
        return _col_reduce(acc, jnp.sum)

    def radix_pass(b, carry):
        prefix, cnt_ge = carry
        cand = prefix | jnp.left_shift(jnp.int32(1), 31 - b)
        cand_s = cand ^ INT_MIN
        cnt = count(lambda k, _: jnp.where(k >= cand_s, 1, 0))
        ok = cnt >= topk
        return jnp.where(ok, cand, prefix), jnp.where(ok, cnt, cnt_ge)

    prefix0 = jnp.zeros((1, q_blk), I32)
    total = jnp.zeros((1, q_blk), I32) + n_tiles * tk
    prefix, cnt_ge = lax.fori_loop(0, 32, radix_pass, (prefix0, total))
    kth = prefix ^ INT_MIN
    thr = jnp.maximum(kth, MIN_VALID_KEY)

    has_tie = jnp.where(cnt_ge > topk, jnp.where(kth >= MIN_VALID_KEY, 1, 0), 0)

    @pl.when(jnp.max(has_tie) > 0)
    def _():
        n_gt = count(lambda k, _: jnp.where(k > thr, 1, 0))
        need = topk - n_gt

        def idx_pass(b, pre):
            cand = pre | jnp.left_shift(jnp.int32(1), idx_bits - 1 - b)
            c = count(lambda k, kidx: jnp.where(k == thr, jnp.where(kidx < cand, 1, 0), 0))
            return jnp.where(c < need, cand, pre)

        last = lax.fori_loop(0, idx_bits, idx_pass, jnp.zeros((1, q_blk), I32))

        def drop_tile(t, _):
            off = pl.multiple_of(t * tk, tk)
            k = keys_scr[pl.ds(off, tk), :]
            drop = jnp.where(k == thr, jnp.where(off + row_iota > last, INT_MIN, k), k)
            keys_scr[pl.ds(off, tk), :] = drop
            return 0

        lax.fori_loop(0, n_tiles, drop_tile, 0)

    acc_scr[...] = jnp.zeros_like(acc_scr)

    heads = [slice(h * q_blk, (h + 1) * q_blk) for h in range(N_HEADS)]

    def lane_row(x8):
        return jnp.concatenate([x8[h:h + 1, :] for h in range(N_HEADS)], axis=1)

    def tile_operands(t):
        off = pl.multiple_of(t * tk, tk)
        bias = jnp.where(keys_scr[pl.ds(off, tk), :] >= thr, 0.0, NEG_INF)
        return kk_ref[pl.ds(off, tk), :], bias, vat_ref[:, pl.ds(off, tk)]

    def attn_tile(t, m_all):
        kt, bias, vt = tile_operands(t)
        set_row_term(jnp.zeros((1, width), F32))
        logits = jnp.dot(kt, rhs_scr[...], preferred_element_type=F32)
        m_rows = []
        for h in range(N_HEADS):
            s = logits[:, heads[h]] + bias
            m_new = jnp.maximum(m_all[h:h + 1, :], _col_reduce(s, jnp.max))
            p_scr[:, heads[h]] = jnp.exp(s - m_new).astype(BF16)
            m_rows.append(m_new)
        m_new = jnp.concatenate(m_rows, axis=0)
        pv = jnp.dot(vt, p_scr[...], preferred_element_type=F32)
        acc_scr[...] = acc_scr[...] * lane_row(jnp.exp(m_all - m_new)) + pv
        return m_new

    def one_pass_tile(t, carry):
        m_ref, worst = carry
        kt, bias, vt = tile_operands(t)
        set_row_term(-lane_row(m_ref))
        x_all = jnp.dot(kt, rhs_scr[...], preferred_element_type=F32)
        tmax = []
        for h in range(N_HEADS):
            x = x_all[:, heads[h]] + bias
            p_scr[:, heads[h]] = jnp.exp(x).astype(BF16)
            tmax.append(_col_reduce(x, jnp.max))
        tmax = jnp.concatenate(tmax, axis=0)
        up = jnp.maximum(tmax, 0.0)
        pv = jnp.dot(vt, p_scr[...], preferred_element_type=F32)
        acc_scr[...] = (acc_scr[...] + pv) * lane_row(jnp.exp(-up))
        return m_ref + up, jnp.maximum(worst, tmax)

    m0 = jnp.full((N_HEADS, q_blk), M_INIT, F32)
    m1 = attn_tile(0, m0)
    _, worst = lax.fori_loop(1, n_tiles, one_pass_tile, (m1, m0))

    @pl.when(jnp.max(worst) > MAX_EXP_ARG)
    def _():
        acc_scr[...] = jnp.zeros_like(acc_scr)
        lax.fori_loop(0, n_tiles, attn_tile, m0)

    denom = acc_scr[LANE_ONES:LANE_ONES + 1, :]
    for pair in range(N_HEADS // 2):
        halves = []
        for h in (2 * pair, 2 * pair + 1):
            halves.append(acc_scr[0:HEAD_DIM, heads[h]] / denom[:, heads[h]])
        o_ref[:, pair * LANES:(pair + 1) * LANES] = (
            jnp.concatenate(halves, axis=0).T.astype(BF16))


def _dsa_call(qit, qat, auxt, kk, vat, *, q_blk, tk):
    B, _, _, S = qit.shape
    topk = min(TOPK_MAX, S // 4)
    idx_bits = max(1, int(np.ceil(np.log2(S))))
    width = N_HEADS * q_blk
    hm_spec = pl.BlockSpec((None, N_HEADS, HEAD_DIM, q_blk), lambda b, i: (b, 0, 0, i))
    return pl.pallas_call(
        functools.partial(_dsa_kernel, q_blk=q_blk, tk=tk, topk=topk, idx_bits=idx_bits),
        grid=(B, S // q_blk),
        in_specs=[
            hm_spec, hm_spec,
            pl.BlockSpec((None, 2 * N_HEADS, q_blk), lambda b, i: (b, 0, i)),
            pl.BlockSpec((None, S, 2 * LANES), lambda b, i: (b, 0, 0)),
            pl.BlockSpec((None, LANES, S), lambda b, i: (b, 0, 0)),
        ],
        out_specs=pl.BlockSpec((None, q_blk, W_HEADS), lambda b, i: (b, i, 0)),
        out_shape=jax.ShapeDtypeStruct((B, S, W_HEADS), BF16),
        scratch_shapes=[
            pltpu.VMEM((S, q_blk), I32),
            pltpu.VMEM((LANES, width), F32),
            pltpu.VMEM((tk, width), BF16),
            pltpu.VMEM((2 * LANES, width), BF16),
        ],
        compiler_params=_params(2),
        name="dsa",
    )(qit, qat, auxt, kk, vat)


def _fox_kernel(qi_tbl, kj_tbl, qt_ref, k_ref, vt_ref, auxq_ref, neg_ref, o_ref,
                m_scr, acc_scr, pv_scr, done_scr, *, t):
    step = pl.program_id(1)
    i = qi_tbl[step]
    j = kj_tbl[step]
    top = lax.broadcasted_iota(I32, (LANES, t), 0) < HEAD_DIM
    piece_row = lax.broadcasted_iota(I32, (PIECE_ROWS, t), 0)
    ones_rows = jnp.ones((PIECE_ROWS, t), BF16)

    @pl.when(j == 0)
    def _():
        m_scr[...] = jnp.full_like(m_scr, M_INIT)
        acc_scr[...] = jnp.zeros_like(acc_scr)

    def head_operands(h, r):
        pair, e = divmod(h, 2)
        qp = qt_ref[pair * LANES:(pair + 1) * LANES, :]
        qm = jnp.where(top, qp, 0) if e == 0 else jnp.where(top, 0, qp)
        hi = r.astype(BF16).astype(F32)
        mid = (r - hi).astype(BF16).astype(F32)
        lo = r - hi - mid
        pieces = jnp.where(piece_row == 0, hi, jnp.where(piece_row == 1, mid,
                                                         jnp.where(piece_row == 2, lo, 0.0)))
        rhs = jnp.concatenate([qm, pieces.astype(BF16), neg_ref[h]], axis=0)
        v_ext = jnp.concatenate([vt_ref[h * HEAD_DIM:(h + 1) * HEAD_DIM, :], ones_rows], axis=0)
        return k_ref[:, 2 * pair * LANES:(2 * pair + 2) * LANES], rhs, v_ext

    def two_pass_body():
        causal = (j * t + lax.broadcasted_iota(I32, (t, t), 0)
                  <= i * t + lax.broadcasted_iota(I32, (t, t), 1))
        for h in range(N_HEADS):
            lhs, rhs, v_ext = head_operands(h, auxq_ref[N_HEADS + h:N_HEADS + h + 1, :])
            s = jnp.dot(lhs, rhs, preferred_element_type=F32)
            s = jnp.where(causal, s, NEG_INF)
            m_prev = m_scr[h:h + 1, :]
            m_new = jnp.maximum(m_prev, _col_reduce(s, jnp.max))
            p = jnp.exp(s - m_new).astype(BF16)
            m_scr[h:h + 1, :] = m_new
            acc_scr[h] = (acc_scr[h] * jnp.exp(m_prev - m_new)
                          + jnp.dot(v_ext, p, preferred_element_type=F32))

    def one_pass_body(diag):
        if diag:
            causal = (lax.broadcasted_iota(I32, (t, t), 0)
                      <= lax.broadcasted_iota(I32, (t, t), 1))
        tmax = []
        for h in range(N_HEADS):
            r = auxq_ref[N_HEADS + h:N_HEADS + h + 1, :] - m_scr[h:h + 1, :]
            lhs, rhs, v_ext = head_operands(h, r)
            x = jnp.dot(lhs, rhs, preferred_element_type=F32)
            if diag:
                x = jnp.where(causal, x, NEG_INF)
            tmax.append(_col_reduce(x, jnp.max))
            pv_scr[h] = jnp.dot(v_ext, jnp.exp(x).astype(BF16), preferred_element_type=F32)
        worst = jnp.max(jnp.concatenate(tmax, axis=0))
        ok = worst <= MAX_EXP_ARG

        @pl.when(ok)
        def _():
            for h in range(N_HEADS):
                up = jnp.maximum(tmax[h], 0.0)
                m_scr[h:h + 1, :] = m_scr[h:h + 1, :] + up
                acc_scr[h] = (acc_scr[h] + pv_scr[h]) * jnp.exp(-up)

        return jnp.where(ok, 1, 0)

    done_scr[0] = 0

    @pl.when(jnp.logical_and(j > 0, j < i))
    def _():
        done_scr[0] = one_pass_body(False)

    @pl.when(jnp.logical_and(j > 0, j == i))
    def _():
        done_scr[0] = one_pass_body(True)

    @pl.when(done_scr[0] == 0)
    def _():
        two_pass_body()

    @pl.when(j == i)
    def _():
        for pair in range(N_HEADS // 2):
            halves = [acc_scr[h, 0:HEAD_DIM, :] / acc_scr[h, HEAD_DIM:HEAD_DIM + 1, :]
                      for h in (2 * pair, 2 * pair + 1)]
            o_ref[:, pair * LANES:(pair + 1) * LANES] = (
                jnp.concatenate(halves, axis=0).T.astype(BF16))


def _fox_call(qt, k, vt, auxt, *, t):
    B, S, _ = k.shape
    n = S // t
    neg = np.zeros((N_HEADS, LANES - PIECE_ROWS, t), np.float32)
    for h in range(N_HEADS):
        neg[h, N_PIECES * h:N_PIECES * (h + 1), :] = -1.0
    qi_tbl = np.concatenate([np.full(i + 1, i, np.int32) for i in range(n)])
    kj_tbl = np.concatenate([np.arange(i + 1, dtype=np.int32) for i in range(n)])
    grid_spec = pltpu.PrefetchScalarGridSpec(
        num_scalar_prefetch=2,
        grid=(B, len(qi_tbl)),
        in_specs=[
            pl.BlockSpec((None, W_HEADS, t), lambda b, s, qi, kj: (b, 0, qi[s])),
            pl.BlockSpec((None, t, 2 * W_HEADS), lambda b, s, qi, kj: (b, kj[s], 0)),
            pl.BlockSpec((None, W_HEADS, t), lambda b, s, qi, kj: (b, 0, kj[s])),
            pl.BlockSpec((None, 2 * N_HEADS, t), lambda b, s, qi, kj: (b, 0, qi[s])),
            pl.BlockSpec((N_HEADS, LANES - PIECE_ROWS, t), lambda b, s, qi, kj: (0, 0, 0)),
        ],
        out_specs=pl.BlockSpec((None, t, W_HEADS), lambda b, s, qi, kj: (b, qi[s], 0)),
        scratch_shapes=[
            pltpu.VMEM((N_HEADS, t), F32),
            pltpu.VMEM((N_HEADS, HEAD_DIM + PIECE_ROWS, t), F32),
            pltpu.VMEM((N_HEADS, HEAD_DIM + PIECE_ROWS, t), F32),
            pltpu.SMEM((1,), I32),
        ],
    )
    return pl.pallas_call(
        functools.partial(_fox_kernel, t=t),
        grid_spec=grid_spec,
        out_shape=jax.ShapeDtypeStruct((B, S, W_HEADS), BF16),
        compiler_params=_params(2),
        name="fox",
    )(jnp.asarray(qi_tbl), jnp.asarray(kj_tbl), qt, k, vt, auxt, jnp.asarray(neg, BF16))


def _merge_kernel(oa_ref, ob_ref, ga_ref, gb_ref, x_ref, g1_ref, sc_ref, sh_ref, n2_ref,
                  wa_ref, wb_ref, wo_ref, x1_ref, h2_ref):
    ya = jnp.dot(oa_ref[...], wa_ref[...], preferred_element_type=F32)
    yb = jnp.dot(ob_ref[...], wb_ref[...], preferred_element_type=F32)
    mix = ga_ref[...].astype(F32) * ya + gb_ref[...].astype(F32) * yb
    y = jnp.dot(mix.astype(BF16), wo_ref[...], preferred_element_type=F32)
    x1 = x_ref[...] + g1_ref[...] * y
    x1_ref[...] = x1
    h2_ref[...] = _rms_mod(x1, n2_ref[...], sc_ref[...], sh_ref[...]).astype(BF16)


def _merge_call(oa, ob, ga, gb, x, g1, sc2, sh2, n2, wa, wb, wo, *, tm):
    B, S, _ = x.shape
    row = lambda b, s: (b, s, 0)
    const2 = lambda b, s: (0, 0)
    vec = pl.BlockSpec((None, 1, D_MODEL), lambda b, s: (b, 0, 0))
    return pl.pallas_call(
        _merge_kernel,
        grid=(B, S // tm),
        in_specs=[
            pl.BlockSpec((None, tm, W_HEADS), row),
            pl.BlockSpec((None, tm, W_HEADS), row),
            pl.BlockSpec((None, tm, D_MODEL), row),
            pl.BlockSpec((None, tm, D_MODEL), row),
            pl.BlockSpec((None, tm, D_MODEL), row),
            vec, vec, vec,
            pl.BlockSpec((1, D_MODEL), const2),
            pl.BlockSpec((W_HEADS, D_MODEL), const2),
            pl.BlockSpec((W_HEADS, D_MODEL), const2),
            pl.BlockSpec((D_MODEL, D_MODEL), const2),
        ],
        out_specs=[pl.BlockSpec((None, tm, D_MODEL), row),
                   pl.BlockSpec((None, tm, D_MODEL), row)],
        out_shape=[jax.ShapeDtypeStruct((B, S, D_MODEL), F32),
                   jax.ShapeDtypeStruct((B, S, D_MODEL), BF16)],
        compiler_params=_params(2),
        name="merge",
    )(oa, ob, ga, gb, x, g1, sc2, sh2, n2, wa, wb, wo)


def _ffn_kernel(h_ref, x_ref, g2_ref, wup_ref, cw_ref, cb_ref, wdn_ref, fg_ref, o_ref,
                carry_scr, *, tm, fc, final):
    s_idx = pl.program_id(1)

    @pl.when(s_idx == 0)
    def _():
        carry_scr[...] = jnp.zeros_like(carry_scr)

    hb = h_ref[...]
    row = lax.broadcasted_iota(I32, (tm, fc), 0)
    acc = jnp.zeros((tm, D_MODEL), F32)
    for cblk in range(D_FF // fc):
        halves = []
        for part in range(2):
            col = part * D_FF + cblk * fc
            u = jnp.dot(hb, wup_ref[:, col:col + fc], preferred_element_type=F32)
            prev = carry_scr[:, col:col + fc]
            u1 = jnp.where(row == 0, prev[7:8], pltpu.roll(u, 1, 0))
            u2 = jnp.where(row == 0, prev[6:7],
                           jnp.where(row == 1, prev[7:8], pltpu.roll(u, 2, 0)))
            carry_scr[:, col:col + fc] = u[tm - SUBLANES:tm]
            cw = cw_ref[:, col:col + fc]
            halves.append(cb_ref[:, col:col + fc] + (cw[0:1] * u2 + cw[1:2] * u1 + cw[2:3] * u))
        act = (jax.nn.silu(halves[0]) * halves[1]).astype(BF16)
        acc = acc + jnp.dot(act, wdn_ref[cblk * fc:(cblk + 1) * fc, :],
                            preferred_element_type=F32)
    x2 = x_ref[...] + g2_ref[...] * acc
    if final:
        var = jnp.mean(x2 * x2, axis=-1, keepdims=True)
        x2 = x2 * lax.rsqrt(var + EPS) * fg_ref[...]
    o_ref[...] = x2


def _ffn_call(h2, x1, g2, wup, cw, cb, wdn, fg, *, tm, fc, final):
    B, S, _ = x1.shape
    row = lambda b, s: (b, s, 0)
    const2 = lambda b, s: (0, 0)
    return pl.pallas_call(
        functools.partial(_ffn_kernel, tm=tm, fc=fc, final=final),
        grid=(B, S // tm),
        in_specs=[
            pl.BlockSpec((None, tm, D_MODEL), row),
            pl.BlockSpec((None, tm, D_MODEL), row),
            pl.BlockSpec((None, 1, D_MODEL), lambda b, s: (b, 0, 0)),
            pl.BlockSpec((D_MODEL, 2 * D_FF), const2, pipeline_mode=pl.Buffered(1)),
            pl.BlockSpec((SUBLANES, 2 * D_FF), const2),
            pl.BlockSpec((1, 2 * D_FF), const2),
            pl.BlockSpec((D_FF, D_MODEL), const2, pipeline_mode=pl.Buffered(1)),
            pl.BlockSpec((1, D_MODEL), const2),
        ],
        out_specs=pl.BlockSpec((None, tm, D_MODEL), row),
        out_shape=jax.ShapeDtypeStruct((B, S, D_MODEL), F32),
        scratch_shapes=[pltpu.VMEM((SUBLANES, 2 * D_FF), F32)],
        compiler_params=_params(2),
        name="ffn",
    )(h2, x1, g2, wup, cw, cb, wdn, fg)


def _reorder_w_in(w):
    o = np.cumsum([0, W_HEADS, HEAD_DIM, HEAD_DIM, W_HEADS, HEAD_DIM, N_HEADS,
                   W_HEADS, W_HEADS, W_HEADS, N_HEADS, 2 * D_MODEL])
    seg = lambda k: w[:, o[k]:o[k + 1]]
    q_a, k_a, v_a, q_i, k_i, w_i, q_f, k_f, v_f, f_f, gates = [seg(k) for k in range(11)]
    pad = jnp.zeros((w.shape[0], N_COLS - int(o[-1])), w.dtype)
    return jnp.concatenate([q_a, q_i, q_f, k_f, v_f, gates, k_i, k_a, v_a, w_i, f_f, pad],
                           axis=1).astype(BF16)


def kernel(x, c, positions, mod_w, mod_b, norm1_g, norm2_g, w_in, forget_bias, w_branch_a,
           w_branch_b, w_out, w_up, conv_w, conv_b, w_down, final_g):
    B, S, _ = x.shape
    depth = mod_w.shape[0]
    tm = min(TOKEN_TILE, S)
    mod = _mod_call(c, mod_w, mod_b)[:, :, :B].reshape(depth, 6, B, 1, D_MODEL)
    pos = positions.astype(F32).reshape(B, S, 1)

    inv_freq = ROPE_THETA ** (-jnp.arange(0, ROT_DIM, 2, dtype=F32) / ROT_DIM)
    l64 = np.arange(LANES) % HEAD_DIM
    invf = jnp.where(l64 < ROT_DIM, inv_freq[l64 % (ROT_DIM // 2)], 0.0).reshape(1, LANES)

    place = np.zeros((N_PIECES, LANES, LANES), np.float32)
    for h in range(N_HEADS):
        for piece in range(N_PIECES):
            place[piece, LANE_FORGET + h, LANE_CK + N_PIECES * h + piece] = 1.0
    place = jnp.asarray(place, BF16)

    for l in range(depth):
        sh1, sc1, g1, sh2, sc2, g2 = [mod[l, k] for k in range(6)]
        fb = jnp.zeros((1, LANES), F32).at[0, LANE_FORGET:LANE_FORGET + N_HEADS].set(forget_bias[l])
        (qat, qit, qft, kf, vft, ga, gb, kk, vat, auxt) = _in_call(
            x, sh1, sc1, norm1_g[l].reshape(1, D_MODEL), pos, invf, fb, place,
            _reorder_w_in(w_in[l]), tm=tm)
        oa = _dsa_call(qit, qat, auxt, kk, vat, q_blk=DSA_Q_BLOCK, tk=min(DSA_KEY_TILE, S))
        ob = _fox_call(qft, kf, vft, auxt, t=tm)
        x1, h2 = _merge_call(oa, ob, ga, gb, x, g1, sc2, sh2, norm2_g[l].reshape(1, D_MODEL),
                             w_branch_a[l].astype(BF16), w_branch_b[l].astype(BF16),
                             w_out[l].astype(BF16), tm=tm)
        cw8 = jnp.zeros((SUBLANES, 2 * D_FF), F32).at[:conv_w.shape[1]].set(conv_w[l])
        x = _ffn_call(h2, x1, g2, w_up[l].astype(BF16), cw8, conv_b[l].reshape(1, 2 * D_FF),
                      w_down[l].astype(BF16), final_g.reshape(1, D_MODEL),
                      tm=tm, fc=FFN_CHUNK, final=(l == depth - 1))
    return x
```

```python
import functools

import numpy as np
import jax
import jax.numpy as jnp
from jax import lax
from jax.experimental import pallas as pl
from jax.experimental.pallas import tpu as pltpu

F32 = jnp.float32
BF16 = jnp.bfloat16
I32 = jnp.int32
I16 = jnp.int16

D_MODEL = 1024
CHUNK = 64
HEAD_DIM = 64
N_HEADS = 8
TOPK_MAX = 256
ROPE_THETA = 500000.0
ROT_DIM = HEAD_DIM // 4
D_FF = 2816
EPS = 1e-6
NEG_INF = -1e30
W_HEADS = N_HEADS * HEAD_DIM

LANES = 128
SUBLANES = 8
VMEM_LIMIT = 56 * 1024 * 1024

TOKEN_TILE = 512
DSA_Q_BLOCK = LANES
DSA_KEY_TILE = 512
FFN_CHUNK = 256

COL_QA = 0
COL_QI = 512
COL_QF = 1024
COL_KF = 1536
COL_VF = 2048
COL_GATES = 2560
COL_SMALL = 4608
N_COLS = 4864
LANE_WIDX = 64
LANE_FORGET = 72
LANE_ONES = 80
N_PIECES = 3
PIECE_ROWS = 16
LANE_CK = PIECE_ROWS

INT_MIN = -(2 ** 31)
HALF_RANGE = 2 ** 15
M_INIT = -3.0e38
MAX_EXP_ARG = 60.0


def _f32_key(v):
    b = int(np.float32(v).view(np.int32))
    return b ^ ((b >> 31) & 0x7FFFFFFF)


MIN_VALID_KEY = _f32_key(0.5 * NEG_INF) + 1


def _params(n_grid):
    return pltpu.CompilerParams(
        dimension_semantics=("arbitrary",) * n_grid, vmem_limit_bytes=VMEM_LIMIT)


def _col_reduce(x, op):
    rows = x.shape[0]
    part = op(x.reshape(rows // SUBLANES, SUBLANES, x.shape[1]), axis=0)
    return op(part, axis=0, keepdims=True)


def _mod_kernel(c_ref, w_ref, b_ref, o_ref):
    o_ref[...] = jnp.dot(c_ref[...], w_ref[...], precision=lax.Precision.HIGHEST,
                         preferred_element_type=F32) + b_ref[...]


def _mod_call(c, mod_w, mod_b):
    depth = mod_w.shape[0]
    c8 = jnp.zeros((SUBLANES, D_MODEL), F32).at[: c.shape[0]].set(c)
    b4 = mod_b.reshape(depth, 6, 1, D_MODEL)
    return pl.pallas_call(
        _mod_kernel,
        grid=(depth, 6),
        in_specs=[
            pl.BlockSpec((SUBLANES, D_MODEL), lambda l, j: (0, 0)),
            pl.BlockSpec((None, D_MODEL, D_MODEL), lambda l, j: (l, 0, j)),
            pl.BlockSpec((None, None, 1, D_MODEL), lambda l, j: (l, j, 0, 0)),
        ],
        out_specs=pl.BlockSpec((None, None, SUBLANES, D_MODEL), lambda l, j: (l, j, 0, 0)),
        out_shape=jax.ShapeDtypeStruct((depth, 6, SUBLANES, D_MODEL), F32),
        compiler_params=_params(2),
        name="mod",
    )(c8, mod_w, b4)


def _rms_mod(x, gain, scale, shift):
    var = jnp.mean(x * x, axis=-1, keepdims=True)
    y = x * lax.rsqrt(var + EPS) * gain
    return y * (1.0 + scale) + shift


def _split3(x):
    p0 = x.astype(BF16)
    r1 = x - p0.astype(F32)
    p1 = r1.astype(BF16)
    return p0, p1, (r1 - p1.astype(F32)).astype(BF16)


def _in_kernel(x_ref, sh_ref, sc_ref, g_ref, pos_ref, invf_ref, fb_ref, place_ref, w_ref,
               qat_ref, qit_ref, qft_ref, kf_ref, vft_ref, ga_ref, gb_ref,
               kk_ref, vat_ref, auxt_ref, carry_scr, *, tm):
    s_idx = pl.program_id(1)

    @pl.when(s_idx == 0)
    def _():
        carry_scr[...] = jnp.zeros_like(carry_scr)

    h = _rms_mod(x_ref[...], g_ref[...], sc_ref[...], sh_ref[...])
    hb = h.astype(BF16)

    ang = pos_ref[...] * invf_ref[...]
    cos = jnp.cos(ang)
    sin = jnp.sin(ang)
    l64 = lax.broadcasted_iota(I32, (tm, LANES), 1) & (HEAD_DIM - 1)
    half = ROT_DIM // 2
    t_cos = jnp.where(l64 < ROT_DIM, cos, 1.0)
    t_lo = jnp.where(l64 < half, -sin, 0.0)
    t_hi = jnp.where(l64 < half, 0.0, jnp.where(l64 < ROT_DIM, sin, 0.0))

    def rope(xc):
        return (xc * t_cos + pltpu.roll(xc, LANES - half, 1) * t_lo
                + pltpu.roll(xc, half, 1) * t_hi)

    def proj(col, width):
        return jnp.dot(hb, w_ref[:, col:col + width], preferred_element_type=F32)

    for col, out_ref in ((COL_QA, qat_ref), (COL_QI, qit_ref)):
        p = proj(col, W_HEADS)
        for cblk in range(W_HEADS // LANES):
            r = (rope(p[:, cblk * LANES:(cblk + 1) * LANES]) * 0.125).T.astype(BF16)
            out_ref[2 * cblk] = r[:HEAD_DIM]
            out_ref[2 * cblk + 1] = r[HEAD_DIM:]

    for col, out_ref, scale in ((COL_QF, qft_ref, 0.125), (COL_VF, vft_ref, 1.0)):
        p = proj(col, W_HEADS)
        for cblk in range(W_HEADS // LANES):
            sl = slice(cblk * LANES, (cblk + 1) * LANES)
            out_ref[sl, :] = (p[:, sl] * scale).T.astype(BF16)
    ga_ref[...] = jax.nn.sigmoid(proj(COL_GATES, D_MODEL)).astype(BF16)
    gb_ref[...] = jax.nn.sigmoid(proj(COL_GATES + D_MODEL, D_MODEL)).astype(BF16)

    lane = lax.broadcasted_iota(I32, (tm, LANES), 1)
    ones_lanes = jnp.where(lane < N_PIECES, 1.0, 0.0)
    small = proj(COL_SMALL, 2 * LANES)
    kk_ref[...] = jnp.concatenate([rope(small[:, :LANES]), ones_lanes], axis=1).astype(BF16)
    blk = small[:, LANES:]

    is_f = (lane >= LANE_FORGET) & (lane < LANE_FORGET + N_HEADS)
    z = blk + fb_ref[...]
    logf = jnp.minimum(z, 0.0) - jnp.log1p(jnp.exp(-jnp.abs(z)))
    logf = jnp.where(is_f, logf, 0.0)
    tri = (lax.broadcasted_iota(I32, (tm, tm), 0)
           >= lax.broadcasted_iota(I32, (tm, tm), 1)).astype(BF16)
    p0, p1, p2 = _split3(logf)
    cum = (jnp.dot(tri, p0, preferred_element_type=F32)
           + jnp.dot(tri, p1, preferred_element_type=F32)
           + jnp.dot(tri, p2, preferred_element_type=F32)) + carry_scr[0:1, :]
    carry_scr[0:1, :] = cum[tm - 1:tm, :]

    c0, c1, c2 = _split3(cum)
    kaux = (jnp.dot(c0, place_ref[0], preferred_element_type=F32)
            + jnp.dot(c1, place_ref[1], preferred_element_type=F32)
            + jnp.dot(c2, place_ref[2], preferred_element_type=F32) + ones_lanes).astype(BF16)
    kf = proj(COL_KF, W_HEADS).astype(BF16)
    for pair in range(N_HEADS // 2):
        kf_ref[:, 2 * pair * LANES:(2 * pair + 1) * LANES] = kf[:, pair * LANES:(pair + 1) * LANES]
        kf_ref[:, (2 * pair + 1) * LANES:(2 * pair + 2) * LANES] = kaux

    is_w = (lane >= LANE_WIDX) & (lane < LANE_WIDX + N_HEADS)
    comb = jnp.where(is_w, blk * (N_HEADS ** -0.5), jnp.where(is_f, cum, blk))
    comb_t = jnp.where(lane == LANE_ONES, 1.0, comb).T
    vat_ref[...] = comb_t.astype(BF16)
    auxt_ref[...] = comb_t[LANE_WIDX:LANE_WIDX + 2 * N_HEADS]


def _in_call(x, sh, sc, gain, pos, invf, fb, place, w_all, *, tm):
    B, S, _ = x.shape
    hm_t = jax.ShapeDtypeStruct((B, N_HEADS, HEAD_DIM, S), BF16)
    wide_t = jax.ShapeDtypeStruct((B, W_HEADS, S), BF16)
    gate = jax.ShapeDtypeStruct((B, S, D_MODEL), BF16)
    row = lambda b, s: (b, s, 0)
    col = lambda b, s: (b, 0, s)
    vec = pl.BlockSpec((None, 1, D_MODEL), lambda b, s: (b, 0, 0))
    const2 = lambda b, s: (0, 0)
    hm_spec = pl.BlockSpec((None, N_HEADS, HEAD_DIM, tm), lambda b, s: (b, 0, 0, s))
    return pl.pallas_call(
        functools.partial(_in_kernel, tm=tm),
        grid=(B, S // tm),
        in_specs=[
            pl.BlockSpec((None, tm, D_MODEL), row),
            vec, vec,
            pl.BlockSpec((1, D_MODEL), const2),
            pl.BlockSpec((None, tm, 1), row),
            pl.BlockSpec((1, LANES), const2),
            pl.BlockSpec((1, LANES), const2),
            pl.BlockSpec((N_PIECES, LANES, LANES), lambda b, s: (0, 0, 0)),
            pl.BlockSpec((D_MODEL, N_COLS), const2),
        ],
        out_specs=[
            hm_spec, hm_spec,
            pl.BlockSpec((None, W_HEADS, tm), col),
            pl.BlockSpec((None, tm, 2 * W_HEADS), row),
            pl.BlockSpec((None, W_HEADS, tm), col),
            pl.BlockSpec((None, tm, D_MODEL), row),
            pl.BlockSpec((None, tm, D_MODEL), row),
            pl.BlockSpec((None, tm, 2 * LANES), row),
            pl.BlockSpec((None, LANES, tm), col),
            pl.BlockSpec((None, 2 * N_HEADS, tm), col),
        ],
        out_shape=[hm_t, hm_t, wide_t,
                   jax.ShapeDtypeStruct((B, S, 2 * W_HEADS), BF16),
                   wide_t, gate, gate,
                   jax.ShapeDtypeStruct((B, S, 2 * LANES), BF16),
                   jax.ShapeDtypeStruct((B, LANES, S), BF16),
                   jax.ShapeDtypeStruct((B, 2 * N_HEADS, S), F32)],
        scratch_shapes=[pltpu.VMEM((SUBLANES, LANES), F32)],
        compiler_params=_params(2),
        name="in_proj",
    )(x, sh, sc, gain, pos, invf, fb, place, w_all)


def _dsa_kernel(qit_ref, qat_ref, aux_ref, kk_ref, vat_ref, o_ref,
                keys_scr, acc_scr, p_scr, rhs_scr, hi_scr, lo_scr, *, q_blk, tk, topk, idx_bits):
    i = pl.program_id(1)
    width = N_HEADS * q_blk
    n_tiles = ((i + 1) * q_blk + tk - 1) // tk

    zeros = jnp.zeros((HEAD_DIM, width), BF16)
    qi_ext = jnp.concatenate(
        [jnp.concatenate([qit_ref[h] for h in range(N_HEADS)], axis=1), zeros], axis=0)
    rhs_scr[...] = jnp.concatenate(
        [zeros, jnp.concatenate([qat_ref[h] for h in range(N_HEADS)], axis=1),
         jnp.zeros((LANES, width), BF16)], axis=0)
    piece_row = lax.broadcasted_iota(I32, (PIECE_ROWS, width), 0)

    def set_row_term(r):
        hi = r.astype(BF16).astype(F32)
        mid = (r - hi).astype(BF16).astype(F32)
        lo = r - hi - mid
        blk = jnp.where(piece_row == 0, hi, jnp.where(piece_row == 1, mid,
                                                      jnp.where(piece_row == 2, lo, 0.0)))
        rhs_scr[2 * HEAD_DIM:2 * HEAD_DIM + PIECE_ROWS, :] = blk.astype(BF16)
    w_rows = aux_ref[0:N_HEADS, :]
    q_pos = i * q_blk + lax.broadcasted_iota(I32, (1, q_blk), 1)
    key_lim = (q_pos // CHUNK + 1) * CHUNK
    row_iota = lax.broadcasted_iota(I32, (tk, q_blk), 0)

    def score_tile(t, _):
        off = pl.multiple_of(t * tk, tk)
        s = jnp.dot(kk_ref[pl.ds(off, tk), 0:LANES], qi_ext,
                    preferred_element_type=F32)
        acc = jnp.zeros((tk, q_blk), F32)
        for h in range(N_HEADS):
            acc = acc + w_rows[h:h + 1, :] * jnp.maximum(s[:, h * q_blk:(h + 1) * q_blk], 0.0)
        bits = pltpu.bitcast(acc, I32)
        key = bits ^ ((bits >> 31) & 0x7FFFFFFF)
        key = jnp.where(off + row_iota < key_lim, key, INT_MIN)
        keys_scr[pl.ds(off, tk), :] = key
        hi_scr[pl.ds(off, tk), :] = (key >> 16).astype(I16)
        lo_scr[pl.ds(off, tk), :] = ((key & 0xFFFF) - HALF_RANGE).astype(I16)
        return 0

    lax.fori_loop(0, n_tiles, score_tile, 0)

    part_rows = 64

    def count16(src_ref, pred):
        def tile(t, acc):
            off = pl.multiple_of(t * tk, tk)
            for r in range(tk // part_rows):
                acc = acc + pred(src_ref[pl.ds(off + r * part_rows, part_rows), :])
            return acc
        acc = lax.fori_loop(0, n_tiles, tile, jnp.zeros((part_rows, q_blk), I16))
        return _col_reduce(acc.astype(I32), jnp.sum)

    def spread16(row):
        return jnp.broadcast_to(row, (part_rows, q_blk)).astype(I16)

    one16 = jnp.ones((part_rows, q_blk), I16)
    zero16 = jnp.zeros((part_rows, q_blk), I16)

    def radix16(src_ref, need):
        def radix_pass(b, carry):
            prefix, cnt_at = carry
            cand = prefix | jnp.left_shift(jnp.int32(1), 15 - b)
            cand16 = spread16(cand - HALF_RANGE)
            cnt = count16(src_ref, lambda k: jnp.where(k >= cand16, one16, zero16))
            ok = cnt >= need
            return jnp.where(ok, cand, prefix), jnp.where(ok, cnt, cnt_at)

        total = jnp.zeros((1, q_blk), I32) + n_tiles * tk
        prefix, cnt_at = lax.fori_loop(0, 16, radix_pass, (jnp.zeros((1, q_blk), I32), total))
        return prefix - HALF_RANGE, cnt_at

    hi_k, cnt_hi = radix16(hi_scr, topk)
    hi16 = spread16(hi_k)
    n_above = count16(hi_scr, lambda k: jnp.where(k > hi16, one16, zero16))

    def keep_group(t, _):
        off = pl.multiple_of(t * tk, tk)
        for r in range(tk // part_rows):
            rows = pl.ds(off + r * part_rows, part_rows)
            lo_scr[rows, :] = jnp.where(hi_scr[rows, :] == hi16, lo_scr[rows, :],
                                        jnp.full((part_rows, q_blk), -HALF_RANGE, I16))
        return 0

    lax.fori_loop(0, n_tiles, keep_group, 0)
    lo_k, cnt_lo = radix16(lo_scr, topk - n_above)
    kth = jnp.left_shift(hi_k, 16) | (lo_k + HALF_RANGE)
    cnt_ge = jnp.where(lo_k == -HALF_RANGE, cnt_hi, n_above + cnt_lo)

    def count(pred):
        def tile(t, acc):
            off = pl.multiple_of(t * tk, tk)
            x = pred(keys_scr[pl.ds(off, tk), :], off + row_iota)
            return acc + jnp.sum(x.reshape(tk // 64, 64, q_blk), axis=0)
        acc = lax.fori_loop(0, n_tiles, tile, jnp.zeros((64, q_blk), I32))
        return _col_reduce(acc, jnp.sum)

    thr = jnp.maximum(kth, MIN_VALID_KEY)

    has_tie = jnp.where(cnt_ge > topk, jnp.where(kth >= MIN_VALID_KEY, 1, 0), 0)

    @pl.when(jnp.max(has_tie) > 0)
    def _():
        n_gt = count(lambda k, _: jnp.where(k > thr, 1, 0))
        need = topk - n_gt

        def idx_pass(b, pre):
            cand = pre | jnp.left_shift(jnp.int32(1), idx_bits - 1 - b)
            c = count(lambda k, kidx: jnp.where(k == thr, jnp.where(kidx < cand, 1, 0), 0))
            return jnp.where(c < need, cand, pre)

        last = lax.fori_loop(0, idx_bits, idx_pass, jnp.zeros((1, q_blk), I32))

        def drop_tile(t, _):
            off = pl.multiple_of(t * tk, tk)
            k = keys_scr[pl.ds(off, tk), :]
            drop = jnp.where(k == thr, jnp.where(off + row_iota > last, INT_MIN, k), k)
            keys_scr[pl.ds(off, tk), :] = drop
            return 0

        lax.fori_loop(0, n_tiles, drop_tile, 0)

    acc_scr[...] = jnp.zeros_like(acc_scr)

    heads = [slice(h * q_blk, (h + 1) * q_blk) for h in range(N_HEADS)]

    def lane_row(x8):
        return jnp.concatenate([x8[h:h + 1, :] for h in range(N_HEADS)], axis=1)

    def tile_operands(t):
        off = pl.multiple_of(t * tk, tk)
        bias = jnp.where(keys_scr[pl.ds(off, tk), :] >= thr, 0.0, NEG_INF)
        return kk_ref[pl.ds(off, tk), :], bias, vat_ref[:, pl.ds(off, tk)]

    def attn_tile(t, m_all):
        kt, bias, vt = tile_operands(t)
        set_row_term(jnp.zeros((1, width), F32))
        logits = jnp.dot(kt, rhs_scr[...], preferred_element_type=F32)
        m_rows = []
        for h in range(N_HEADS):
            s = logits[:, heads[h]] + bias
            m_new = jnp.maximum(m_all[h:h + 1, :], _col_reduce(s, jnp.max))
            p_scr[:, heads[h]] = jnp.exp(s - m_new).astype(BF16)
            m_rows.append(m_new)
        m_new = jnp.concatenate(m_rows, axis=0)
        pv = jnp.dot(vt, p_scr[...], preferred_element_type=F32)
        acc_scr[...] = acc_scr[...] * lane_row(jnp.exp(m_all - m_new)) + pv
        return m_new

    def one_pass_tile(t, carry):
        m_ref, worst = carry
        kt, bias, vt = tile_operands(t)
        set_row_term(-lane_row(m_ref))
        x_all = jnp.dot(kt, rhs_scr[...], preferred_element_type=F32)
        tmax = []
        for h in range(N_HEADS):
            x = x_all[:, heads[h]] + bias
            p_scr[:, heads[h]] = jnp.exp(x).astype(BF16)
            tmax.append(_col_reduce(x, jnp.max))
        tmax = jnp.concatenate(tmax, axis=0)
        up = jnp.maximum(tmax, 0.0)
        pv = jnp.dot(vt, p_scr[...], preferred_element_type=F32)
        acc_scr[...] = (acc_scr[...] + pv) * lane_row(jnp.exp(-up))
        return m_ref + up, jnp.maximum(worst, tmax)

    m0 = jnp.full((N_HEADS, q_blk), M_INIT, F32)
    m1 = attn_tile(0, m0)
    _, worst = lax.fori_loop(1, n_tiles, one_pass_tile, (m1, m0))

    @pl.when(jnp.max(worst) > MAX_EXP_ARG)
    def _():
        acc_scr[...] = jnp.zeros_like(acc_scr)
        lax.fori_loop(0, n_tiles, attn_tile, m0)

    denom = acc_scr[LANE_ONES:LANE_ONES + 1, :]
    for pair in range(N_HEADS // 2):
        halves = []
        for h in (2 * pair, 2 * pair + 1):
            halves.append(acc_scr[0:HEAD_DIM, heads[h]] / denom[:, heads[h]])
        o_ref[:, pair * LANES:(pair + 1) * LANES] = (
            jnp.concatenate(halves, axis=0).T.astype(BF16))


def _dsa_call(qit, qat, auxt, kk, vat, *, q_blk, tk):
    B, _, _, S = qit.shape
    topk = min(TOPK_MAX, S // 4)
    idx_bits = max(1, int(np.ceil(np.log2(S))))
    width = N_HEADS * q_blk
    hm_spec = pl.BlockSpec((None, N_HEADS, HEAD_DIM, q_blk), lambda b, i: (b, 0, 0, i))
    return pl.pallas_call(
        functools.partial(_dsa_kernel, q_blk=q_blk, tk=tk, topk=topk, idx_bits=idx_bits),
        grid=(B, S // q_blk),
        in_specs=[
            hm_spec, hm_spec,
            pl.BlockSpec((None, 2 * N_HEADS, q_blk), lambda b, i: (b, 0, i)),
            pl.BlockSpec((None, S, 2 * LANES), lambda b, i: (b, 0, 0)),
            pl.BlockSpec((None, LANES, S), lambda b, i: (b, 0, 0)),
        ],
        out_specs=pl.BlockSpec((None, q_blk, W_HEADS), lambda b, i: (b, i, 0)),
        out_shape=jax.ShapeDtypeStruct((B, S, W_HEADS), BF16),
        scratch_shapes=[
            pltpu.VMEM((S, q_blk), I32),
            pltpu.VMEM((LANES, width), F32),
            pltpu.VMEM((tk, width), BF16),
            pltpu.VMEM((2 * LANES, width), BF16),
            pltpu.VMEM((S, q_blk), I16),
            pltpu.VMEM((S, q_blk), I16),
        ],
        compiler_params=_params(2),
        name="dsa",
    )(qit, qat, auxt, kk, vat)


def _fox_kernel(qi_tbl, kj_tbl, qt_ref, k_ref, vt_ref, auxq_ref, auxk_ref, neg_ref, o_ref,
                m_scr, acc_scr, pv_scr, done_scr, *, t):
    step = pl.program_id(1)
    i = qi_tbl[step]
    j = kj_tbl[step]
    top = lax.broadcasted_iota(I32, (LANES, t), 0) < HEAD_DIM
    piece_row = lax.broadcasted_iota(I32, (PIECE_ROWS, t), 0)
    ones_rows = jnp.ones((PIECE_ROWS, t), BF16)

    @pl.when(j == 0)
    def _():
        m_scr[...] = jnp.full_like(m_scr, M_INIT)
        acc_scr[...] = jnp.zeros_like(acc_scr)

    def head_operands(h, r):
        pair, e = divmod(h, 2)
        qp = qt_ref[pair * LANES:(pair + 1) * LANES, :]
        qm = jnp.where(top, qp, 0) if e == 0 else jnp.where(top, 0, qp)
        hi = r.astype(BF16).astype(F32)
        mid = (r - hi).astype(BF16).astype(F32)
        lo = r - hi - mid
        pieces = jnp.where(piece_row == 0, hi, jnp.where(piece_row == 1, mid,
                                                         jnp.where(piece_row == 2, lo, 0.0)))
        rhs = jnp.concatenate([qm, pieces.astype(BF16), neg_ref[h]], axis=0)
        v_ext = jnp.concatenate([vt_ref[h * HEAD_DIM:(h + 1) * HEAD_DIM, :], ones_rows], axis=0)
        return k_ref[:, 2 * pair * LANES:(2 * pair + 2) * LANES], rhs, v_ext

    def two_pass_body():
        causal = (j * t + lax.broadcasted_iota(I32, (t, t), 0)
                  <= i * t + lax.broadcasted_iota(I32, (t, t), 1))
        for h in range(N_HEADS):
            lhs, rhs, v_ext = head_operands(h, auxq_ref[N_HEADS + h:N_HEADS + h + 1, :])
            s = jnp.dot(lhs, rhs, preferred_element_type=F32)
            s = jnp.where(causal, s, NEG_INF)
            m_prev = m_scr[h:h + 1, :]
            m_new = jnp.maximum(m_prev, _col_reduce(s, jnp.max))
            p = jnp.exp(s - m_new).astype(BF16)
            m_scr[h:h + 1, :] = m_new
            acc_scr[h] = (acc_scr[h] * jnp.exp(m_prev - m_new)
                          + jnp.dot(v_ext, p, preferred_element_type=F32))

    def one_pass_body(diag):
        if diag:
            causal = (lax.broadcasted_iota(I32, (t, t), 0)
                      <= lax.broadcasted_iota(I32, (t, t), 1))
        tmax, shifts = [], []
        for h in range(N_HEADS):
            cq = auxq_ref[N_HEADS + h:N_HEADS + h + 1, :]
            ck_first = auxk_ref[N_HEADS + h:N_HEADS + h + 1, 0:1]
            ck_last = auxk_ref[N_HEADS + h:N_HEADS + h + 1, t - 1:t]
            shift = jnp.minimum(cq - ck_last, 0.0) - jnp.minimum(cq - ck_first, 0.0)
            lhs, rhs, v_ext = head_operands(h, cq - (m_scr[h:h + 1, :] + shift))
            x = jnp.dot(lhs, rhs, preferred_element_type=F32)
            if diag:
                x = jnp.where(causal, x, NEG_INF)
            tmax.append(_col_reduce(x, jnp.max))
            shifts.append(shift)
            pv_scr[h] = jnp.dot(v_ext, jnp.exp(x).astype(BF16), preferred_element_type=F32)
        tmax_all = jnp.concatenate(tmax, axis=0)
        ok = jnp.logical_and(jnp.max(tmax_all) <= MAX_EXP_ARG, jnp.min(tmax_all) >= -MAX_EXP_ARG)

        @pl.when(ok)
        def _():
            for h in range(N_HEADS):
                up = jnp.maximum(tmax[h], 0.0)
                m_scr[h:h + 1, :] = m_scr[h:h + 1, :] + shifts[h] + up
                acc_scr[h] = (acc_scr[h] * jnp.exp(-shifts[h]) + pv_scr[h]) * jnp.exp(-up)

        return jnp.where(ok, 1, 0)

    done_scr[0] = 0

    @pl.when(jnp.logical_and(j > 0, j < i))
    def _():
        done_scr[0] = one_pass_body(False)

    @pl.when(jnp.logical_and(j > 0, j == i))
    def _():
        done_scr[0] = one_pass_body(True)

    @pl.when(done_scr[0] == 0)
    def _():
        two_pass_body()

    @pl.when(j == i)
    def _():
        for pair in range(N_HEADS // 2):
            halves = [acc_scr[h, 0:HEAD_DIM, :] / acc_scr[h, HEAD_DIM:HEAD_DIM + 1, :]
                      for h in (2 * pair, 2 * pair + 1)]
            o_ref[:, pair * LANES:(pair + 1) * LANES] = (
                jnp.concatenate(halves, axis=0).T.astype(BF16))


def _fox_call(qt, k, vt, auxt, *, t):
    B, S, _ = k.shape
    n = S // t
    neg = np.zeros((N_HEADS, LANES - PIECE_ROWS, t), np.float32)
    for h in range(N_HEADS):
        neg[h, N_PIECES * h:N_PIECES * (h + 1), :] = -1.0
    qi_tbl = np.concatenate([np.full(i + 1, i, np.int32) for i in range(n)])
    kj_tbl = np.concatenate([np.arange(i + 1, dtype=np.int32) for i in range(n)])
    grid_spec = pltpu.PrefetchScalarGridSpec(
        num_scalar_prefetch=2,
        grid=(B, len(qi_tbl)),
        in_specs=[
            pl.BlockSpec((None, W_HEADS, t), lambda b, s, qi, kj: (b, 0, qi[s])),
            pl.BlockSpec((None, t, 2 * W_HEADS), lambda b, s, qi, kj: (b, kj[s], 0)),
            pl.BlockSpec((None, W_HEADS, t), lambda b, s, qi, kj: (b, 0, kj[s])),
            pl.BlockSpec((None, 2 * N_HEADS, t), lambda b, s, qi, kj: (b, 0, qi[s])),
            pl.BlockSpec((None, 2 * N_HEADS, t), lambda b, s, qi, kj: (b, 0, kj[s])),
            pl.BlockSpec((N_HEADS, LANES - PIECE_ROWS, t), lambda b, s, qi, kj: (0, 0, 0)),
        ],
        out_specs=pl.BlockSpec((None, t, W_HEADS), lambda b, s, qi, kj: (b, qi[s], 0)),
        scratch_shapes=[
            pltpu.VMEM((N_HEADS, t), F32),
            pltpu.VMEM((N_HEADS, HEAD_DIM + PIECE_ROWS, t), F32),
            pltpu.VMEM((N_HEADS, HEAD_DIM + PIECE_ROWS, t), F32),
            pltpu.SMEM((1,), I32),
        ],
    )
    return pl.pallas_call(
        functools.partial(_fox_kernel, t=t),
        grid_spec=grid_spec,
        out_shape=jax.ShapeDtypeStruct((B, S, W_HEADS), BF16),
        compiler_params=_params(2),
        name="fox",
    )(jnp.asarray(qi_tbl), jnp.asarray(kj_tbl), qt, k, vt, auxt, auxt, jnp.asarray(neg, BF16))


def _merge_kernel(oa_ref, ob_ref, ga_ref, gb_ref, x_ref, g1_ref, sc_ref, sh_ref, n2_ref,
                  wa_ref, wb_ref, wo_ref, x1_ref, h2_ref):
    ya = jnp.dot(oa_ref[...], wa_ref[...], preferred_element_type=F32)
    yb = jnp.dot(ob_ref[...], wb_ref[...], preferred_element_type=F32)
    mix = ga_ref[...].astype(F32) * ya + gb_ref[...].astype(F32) * yb
    y = jnp.dot(mix.astype(BF16), wo_ref[...], preferred_element_type=F32)
    x1 = x_ref[...] + g1_ref[...] * y
    x1_ref[...] = x1
    h2_ref[...] = _rms_mod(x1, n2_ref[...], sc_ref[...], sh_ref[...]).astype(BF16)


def _merge_call(oa, ob, ga, gb, x, g1, sc2, sh2, n2, wa, wb, wo, *, tm):
    B, S, _ = x.shape
    row = lambda b, s: (b, s, 0)
    const2 = lambda b, s: (0, 0)
    vec = pl.BlockSpec((None, 1, D_MODEL), lambda b, s: (b, 0, 0))
    return pl.pallas_call(
        _merge_kernel,
        grid=(B, S // tm),
        in_specs=[
            pl.BlockSpec((None, tm, W_HEADS), row),
            pl.BlockSpec((None, tm, W_HEADS), row),
            pl.BlockSpec((None, tm, D_MODEL), row),
            pl.BlockSpec((None, tm, D_MODEL), row),
            pl.BlockSpec((None, tm, D_MODEL), row),
            vec, vec, vec,
            pl.BlockSpec((1, D_MODEL), const2),
            pl.BlockSpec((W_HEADS, D_MODEL), const2),
            pl.BlockSpec((W_HEADS, D_MODEL), const2),
            pl.BlockSpec((D_MODEL, D_MODEL), const2),
        ],
        out_specs=[pl.BlockSpec((None, tm, D_MODEL), row),
                   pl.BlockSpec((None, tm, D_MODEL), row)],
        out_shape=[jax.ShapeDtypeStruct((B, S, D_MODEL), F32),
                   jax.ShapeDtypeStruct((B, S, D_MODEL), BF16)],
        compiler_params=_params(2),
        name="merge",
    )(oa, ob, ga, gb, x, g1, sc2, sh2, n2, wa, wb, wo)


def _ffn_kernel(h_ref, x_ref, g2_ref, wup_ref, cw_ref, cb_ref, wdn_ref, fg_ref, o_ref,
                carry_scr, *, tm, fc, final):
    s_idx = pl.program_id(1)

    @pl.when(s_idx == 0)
    def _():
        carry_scr[...] = jnp.zeros_like(carry_scr)

    hb = h_ref[...]
    row = lax.broadcasted_iota(I32, (tm, fc), 0)
    acc = jnp.zeros((tm, D_MODEL), F32)
    for cblk in range(D_FF // fc):
        halves = []
        for part in range(2):
            col = part * D_FF + cblk * fc
            u = jnp.dot(hb, wup_ref[:, col:col + fc], preferred_element_type=F32)
            prev = carry_scr[:, col:col + fc]
            u1 = jnp.where(row == 0, prev[7:8], pltpu.roll(u, 1, 0))
            u2 = jnp.where(row == 0, prev[6:7],
                           jnp.where(row == 1, prev[7:8], pltpu.roll(u, 2, 0)))
            carry_scr[:, col:col + fc] = u[tm - SUBLANES:tm]
            cw = cw_ref[:, col:col + fc]
            halves.append(cb_ref[:, col:col + fc] + (cw[0:1] * u2 + cw[1:2] * u1 + cw[2:3] * u))
        act = (jax.nn.silu(halves[0]) * halves[1]).astype(BF16)
        acc = acc + jnp.dot(act, wdn_ref[cblk * fc:(cblk + 1) * fc, :],
                            preferred_element_type=F32)
    x2 = x_ref[...] + g2_ref[...] * acc
    if final:
        var = jnp.mean(x2 * x2, axis=-1, keepdims=True)
        x2 = x2 * lax.rsqrt(var + EPS) * fg_ref[...]
    o_ref[...] = x2


def _ffn_call(h2, x1, g2, wup, cw, cb, wdn, fg, *, tm, fc, final):
    B, S, _ = x1.shape
    row = lambda b, s: (b, s, 0)
    const2 = lambda b, s: (0, 0)
    return pl.pallas_call(
        functools.partial(_ffn_kernel, tm=tm, fc=fc, final=final),
        grid=(B, S // tm),
        in_specs=[
            pl.BlockSpec((None, tm, D_MODEL), row),
            pl.BlockSpec((None, tm, D_MODEL), row),
            pl.BlockSpec((None, 1, D_MODEL), lambda b, s: (b, 0, 0)),
            pl.BlockSpec((D_MODEL, 2 * D_FF), const2, pipeline_mode=pl.Buffered(1)),
            pl.BlockSpec((SUBLANES, 2 * D_FF), const2),
            pl.BlockSpec((1, 2 * D_FF), const2),
            pl.BlockSpec((D_FF, D_MODEL), const2, pipeline_mode=pl.Buffered(1)),
            pl.BlockSpec((1, D_MODEL), const2),
        ],
        out_specs=pl.BlockSpec((None, tm, D_MODEL), row),
        out_shape=jax.ShapeDtypeStruct((B, S, D_MODEL), F32),
        scratch_shapes=[pltpu.VMEM((SUBLANES, 2 * D_FF), F32)],
        compiler_params=_params(2),
        name="ffn",
    )(h2, x1, g2, wup, cw, cb, wdn, fg)


def _reorder_w_in(w):
    o = np.cumsum([0, W_HEADS, HEAD_DIM, HEAD_DIM, W_HEADS, HEAD_DIM, N_HEADS,
                   W_HEADS, W_HEADS, W_HEADS, N_HEADS, 2 * D_MODEL])
    seg = lambda k: w[:, o[k]:o[k + 1]]
    q_a, k_a, v_a, q_i, k_i, w_i, q_f, k_f, v_f, f_f, gates = [seg(k) for k in range(11)]
    pad = jnp.zeros((w.shape[0], N_COLS - int(o[-1])), w.dtype)
    return jnp.concatenate([q_a, q_i, q_f, k_f, v_f, gates, k_i, k_a, v_a, w_i, f_f, pad],
                           axis=1).astype(BF16)


def kernel(x, c, positions, mod_w, mod_b, norm1_g, norm2_g, w_in, forget_bias, w_branch_a,
           w_branch_b, w_out, w_up, conv_w, conv_b, w_down, final_g):
    B, S, _ = x.shape
    depth = mod_w.shape[0]
    tm = min(TOKEN_TILE, S)
    mod = _mod_call(c, mod_w, mod_b)[:, :, :B].reshape(depth, 6, B, 1, D_MODEL)
    pos = positions.astype(F32).reshape(B, S, 1)

    inv_freq = ROPE_THETA ** (-jnp.arange(0, ROT_DIM, 2, dtype=F32) / ROT_DIM)
    l64 = np.arange(LANES) % HEAD_DIM
    invf = jnp.where(l64 < ROT_DIM, inv_freq[l64 % (ROT_DIM // 2)], 0.0).reshape(1, LANES)

    place = np.zeros((N_PIECES, LANES, LANES), np.float32)
    for h in range(N_HEADS):
        for piece in range(N_PIECES):
            place[piece, LANE_FORGET + h, LANE_CK + N_PIECES * h + piece] = 1.0
    place = jnp.asarray(place, BF16)

    for l in range(depth):
        sh1, sc1, g1, sh2, sc2, g2 = [mod[l, k] for k in range(6)]
        fb = jnp.zeros((1, LANES), F32).at[0, LANE_FORGET:LANE_FORGET + N_HEADS].set(forget_bias[l])
        (qat, qit, qft, kf, vft, ga, gb, kk, vat, auxt) = _in_call(
            x, sh1, sc1, norm1_g[l].reshape(1, D_MODEL), pos, invf, fb, place,
            _reorder_w_in(w_in[l]), tm=tm)
        oa = _dsa_call(qit, qat, auxt, kk, vat, q_blk=DSA_Q_BLOCK, tk=min(DSA_KEY_TILE, S))
        ob = _fox_call(qft, kf, vft, auxt, t=tm)
        x1, h2 = _merge_call(oa, ob, ga, gb, x, g1, sc2, sh2, norm2_g[l].reshape(1, D_MODEL),
                             w_branch_a[l].astype(BF16), w_branch_b[l].astype(BF16),
                             w_out[l].astype(BF16), tm=tm)
        cw8 = jnp.zeros((SUBLANES, 2 * D_FF), F32).at[:conv_w.shape[1]].set(conv_w[l])
        x = _ffn_call(h2, x1, g2, w_up[l].astype(BF16), cw8, conv_b[l].reshape(1, 2 * D_FF),
                      w_down[l].astype(BF16), final_g.reshape(1, D_MODEL),
                      tm=tm, fc=FFN_CHUNK, final=(l == depth - 1))
    return x
```

```python
import functools

import numpy as np
import jax
import jax.numpy as jnp
from jax import lax
from jax.experimental import pallas as pl
from jax.experimental.pallas import tpu as pltpu

F32 = jnp.float32
BF16 = jnp.bfloat16
I32 = jnp.int32

D_MODEL = 1024
CHUNK = 64
HEAD_DIM = 64
N_HEADS = 8
TOPK_MAX = 256
ROPE_THETA = 500000.0
ROT_DIM = HEAD_DIM // 4
D_FF = 2816
EPS = 1e-6
NEG_INF = -1e30
W_HEADS = N_HEADS * HEAD_DIM

LANES = 128
SUBLANES = 8
VMEM_LIMIT = 56 * 1024 * 1024

TOKEN_TILE = 512
DSA_Q_BLOCK = LANES
DSA_KEY_TILE = 512
FFN_CHUNK = 256

COL_QA = 0
COL_QI = 512
COL_QF = 1024
COL_KF = 1536
COL_VF = 2048
COL_GATES = 2560
COL_SMALL = 4608
N_COLS = 4864
LANE_WIDX = 64
LANE_FORGET = 72
LANE_ONES = 80
N_PIECES = 3
PIECE_ROWS = 16
LANE_CK = PIECE_ROWS

INT_MIN = -(2 ** 31)
N_BUCKETS = TOPK_MAX
M_INIT = -3.0e38
LOG2E = 1.4426950408889634
MAX_EXP_ARG = 60.0


def _f32_key(v):
    b = int(np.float32(v).view(np.int32))
    return b ^ ((b >> 31) & 0x7FFFFFFF)


MIN_VALID_KEY = _f32_key(0.5 * NEG_INF) + 1


def _params(n_grid):
    return pltpu.CompilerParams(
        dimension_semantics=("arbitrary",) * n_grid, vmem_limit_bytes=VMEM_LIMIT)


def _col_reduce(x, op):
    rows = x.shape[0]
    part = op(x.reshape(rows // SUBLANES, SUBLANES, x.shape[1]), axis=0)
    return op(part, axis=0, keepdims=True)


def _mod_kernel(c_ref, w_ref, b_ref, o_ref):
    o_ref[...] = jnp.dot(c_ref[...], w_ref[...], precision=lax.Precision.HIGHEST,
                         preferred_element_type=F32) + b_ref[...]


def _mod_call(c, mod_w, mod_b):
    depth = mod_w.shape[0]
    c8 = jnp.zeros((SUBLANES, D_MODEL), F32).at[: c.shape[0]].set(c)
    b4 = mod_b.reshape(depth, 6, 1, D_MODEL)
    return pl.pallas_call(
        _mod_kernel,
        grid=(depth, 6),
        in_specs=[
            pl.BlockSpec((SUBLANES, D_MODEL), lambda l, j: (0, 0)),
            pl.BlockSpec((None, D_MODEL, D_MODEL), lambda l, j: (l, 0, j)),
            pl.BlockSpec((None, None, 1, D_MODEL), lambda l, j: (l, j, 0, 0)),
        ],
        out_specs=pl.BlockSpec((None, None, SUBLANES, D_MODEL), lambda l, j: (l, j, 0, 0)),
        out_shape=jax.ShapeDtypeStruct((depth, 6, SUBLANES, D_MODEL), F32),
        compiler_params=_params(2),
        name="mod",
    )(c8, mod_w, b4)


def _rms_mod(x, gain, scale, shift):
    var = jnp.mean(x * x, axis=-1, keepdims=True)
    y = x * lax.rsqrt(var + EPS) * gain
    return y * (1.0 + scale) + shift


def _split3(x):
    p0 = x.astype(BF16)
    r1 = x - p0.astype(F32)
    p1 = r1.astype(BF16)
    return p0, p1, (r1 - p1.astype(F32)).astype(BF16)


def _in_kernel(x_ref, sh_ref, sc_ref, g_ref, pos_ref, invf_ref, fb_ref, place_ref, w_ref,
               qat_ref, qit_ref, qft_ref, kf_ref, vft_ref, ga_ref, gb_ref,
               kk_ref, vat_ref, auxt_ref, carry_scr, *, tm):
    s_idx = pl.program_id(1)

    @pl.when(s_idx == 0)
    def _():
        carry_scr[...] = jnp.zeros_like(carry_scr)

    h = _rms_mod(x_ref[...], g_ref[...], sc_ref[...], sh_ref[...])
    hb = h.astype(BF16)

    ang = pos_ref[...] * invf_ref[...]
    cos = jnp.cos(ang)
    sin = jnp.sin(ang)
    l64 = lax.broadcasted_iota(I32, (tm, LANES), 1) & (HEAD_DIM - 1)
    half = ROT_DIM // 2
    t_cos = jnp.where(l64 < ROT_DIM, cos, 1.0)
    t_lo = jnp.where(l64 < half, -sin, 0.0)
    t_hi = jnp.where(l64 < half, 0.0, jnp.where(l64 < ROT_DIM, sin, 0.0))

    def rope(xc):
        return (xc * t_cos + pltpu.roll(xc, LANES - half, 1) * t_lo
                + pltpu.roll(xc, half, 1) * t_hi)

    def proj(col, width):
        return jnp.dot(hb, w_ref[:, col:col + width], preferred_element_type=F32)

    for col, out_ref, scale in ((COL_QA, qat_ref, 0.125 * LOG2E), (COL_QI, qit_ref, 0.125)):
        p = proj(col, W_HEADS)
        for cblk in range(W_HEADS // LANES):
            r = (rope(p[:, cblk * LANES:(cblk + 1) * LANES]) * scale).T.astype(BF16)
            out_ref[2 * cblk] = r[:HEAD_DIM]
            out_ref[2 * cblk + 1] = r[HEAD_DIM:]

    for col, out_ref, scale in ((COL_QF, qft_ref, 0.125 * LOG2E), (COL_VF, vft_ref, 1.0)):
        p = proj(col, W_HEADS)
        for cblk in range(W_HEADS // LANES):
            sl = slice(cblk * LANES, (cblk + 1) * LANES)
            out_ref[sl, :] = (p[:, sl] * scale).T.astype(BF16)
    ga_ref[...] = jax.nn.sigmoid(proj(COL_GATES, D_MODEL)).astype(BF16)
    gb_ref[...] = jax.nn.sigmoid(proj(COL_GATES + D_MODEL, D_MODEL)).astype(BF16)

    lane = lax.broadcasted_iota(I32, (tm, LANES), 1)
    ones_lanes = jnp.where(lane < N_PIECES, 1.0, 0.0)
    small = proj(COL_SMALL, 2 * LANES)
    kk_ref[...] = jnp.concatenate([rope(small[:, :LANES]), ones_lanes], axis=1).astype(BF16)
    blk = small[:, LANES:]

    is_f = (lane >= LANE_FORGET) & (lane < LANE_FORGET + N_HEADS)
    z = blk + fb_ref[...]
    logf = jnp.minimum(z, 0.0) - jnp.log1p(jnp.exp(-jnp.abs(z)))
    logf = jnp.where(is_f, logf, 0.0)
    tri = (lax.broadcasted_iota(I32, (tm, tm), 0)
           >= lax.broadcasted_iota(I32, (tm, tm), 1)).astype(BF16)
    p0, p1, p2 = _split3(logf)
    cum = (jnp.dot(tri, p0, preferred_element_type=F32)
           + jnp.dot(tri, p1, preferred_element_type=F32)
           + jnp.dot(tri, p2, preferred_element_type=F32)) + carry_scr[0:1, :]
    carry_scr[0:1, :] = cum[tm - 1:tm, :]
    cum = cum * LOG2E

    c0, c1, c2 = _split3(cum)
    kaux = (jnp.dot(c0, place_ref[0], preferred_element_type=F32)
            + jnp.dot(c1, place_ref[1], preferred_element_type=F32)
            + jnp.dot(c2, place_ref[2], preferred_element_type=F32) + ones_lanes).astype(BF16)
    kf = proj(COL_KF, W_HEADS).astype(BF16)
    for pair in range(N_HEADS // 2):
        kf_ref[:, 2 * pair * LANES:(2 * pair + 1) * LANES] = kf[:, pair * LANES:(pair + 1) * LANES]
        kf_ref[:, (2 * pair + 1) * LANES:(2 * pair + 2) * LANES] = kaux

    is_w = (lane >= LANE_WIDX) & (lane < LANE_WIDX + N_HEADS)
    comb = jnp.where(is_w, blk * (N_HEADS ** -0.5), jnp.where(is_f, cum, blk))
    comb_t = jnp.where(lane == LANE_ONES, 1.0, comb).T
    vat_ref[...] = comb_t.astype(BF16)
    auxt_ref[...] = comb_t[LANE_WIDX:LANE_WIDX + 2 * N_HEADS]


def _in_call(x, sh, sc, gain, pos, invf, fb, place, w_all, *, tm):
    B, S, _ = x.shape
    hm_t = jax.ShapeDtypeStruct((B, N_HEADS, HEAD_DIM, S), BF16)
    wide_t = jax.ShapeDtypeStruct((B, W_HEADS, S), BF16)
    gate = jax.ShapeDtypeStruct((B, S, D_MODEL), BF16)
    row = lambda b, s: (b, s, 0)
    col = lambda b, s: (b, 0, s)
    vec = pl.BlockSpec((None, 1, D_MODEL), lambda b, s: (b, 0, 0))
    const2 = lambda b, s: (0, 0)
    hm_spec = pl.BlockSpec((None, N_HEADS, HEAD_DIM, tm), lambda b, s: (b, 0, 0, s))
    return pl.pallas_call(
        functools.partial(_in_kernel, tm=tm),
        grid=(B, S // tm),
        in_specs=[
            pl.BlockSpec((None, tm, D_MODEL), row),
            vec, vec,
            pl.BlockSpec((1, D_MODEL), const2),
            pl.BlockSpec((None, tm, 1), row),
            pl.BlockSpec((1, LANES), const2),
            pl.BlockSpec((1, LANES), const2),
            pl.BlockSpec((N_PIECES, LANES, LANES), lambda b, s: (0, 0, 0)),
            pl.BlockSpec((D_MODEL, N_COLS), const2),
        ],
        out_specs=[
            hm_spec, hm_spec,
            pl.BlockSpec((None, W_HEADS, tm), col),
            pl.BlockSpec((None, tm, 2 * W_HEADS), row),
            pl.BlockSpec((None, W_HEADS, tm), col),
            pl.BlockSpec((None, tm, D_MODEL), row),
            pl.BlockSpec((None, tm, D_MODEL), row),
            pl.BlockSpec((None, tm, 2 * LANES), row),
            pl.BlockSpec((None, LANES, tm), col),
            pl.BlockSpec((None, 2 * N_HEADS, tm), col),
        ],
        out_shape=[hm_t, hm_t, wide_t,
                   jax.ShapeDtypeStruct((B, S, 2 * W_HEADS), BF16),
                   wide_t, gate, gate,
                   jax.ShapeDtypeStruct((B, S, 2 * LANES), BF16),
                   jax.ShapeDtypeStruct((B, LANES, S), BF16),
                   jax.ShapeDtypeStruct((B, 2 * N_HEADS, S), F32)],
        scratch_shapes=[pltpu.VMEM((SUBLANES, LANES), F32)],
        compiler_params=_params(2),
        name="in_proj",
    )(x, sh, sc, gain, pos, invf, fb, place, w_all)


def _dsa_kernel(qit_ref, qat_ref, aux_ref, kk_ref, vat_ref, o_ref,
                keys_scr, acc_scr, p_scr, rhs_scr, bmax_scr, *, q_blk, tk, topk, idx_bits):
    i = pl.program_id(1)
    width = N_HEADS * q_blk
    n_tiles = ((i + 1) * q_blk + tk - 1) // tk

    zeros = jnp.zeros((HEAD_DIM, width), BF16)
    qi_ext = jnp.concatenate(
        [jnp.concatenate([qit_ref[h] for h in range(N_HEADS)], axis=1), zeros], axis=0)
    rhs_scr[...] = jnp.concatenate(
        [zeros, jnp.concatenate([qat_ref[h] for h in range(N_HEADS)], axis=1),
         jnp.zeros((LANES, width), BF16)], axis=0)
    piece_row = lax.broadcasted_iota(I32, (PIECE_ROWS, width), 0)

    def set_row_term(r):
        hi = r.astype(BF16).astype(F32)
        mid = (r - hi).astype(BF16).astype(F32)
        lo = r - hi - mid
        blk = jnp.where(piece_row == 0, hi, jnp.where(piece_row == 1, mid,
                                                      jnp.where(piece_row == 2, lo, 0.0)))
        rhs_scr[2 * HEAD_DIM:2 * HEAD_DIM + PIECE_ROWS, :] = blk.astype(BF16)
    w_rows = aux_ref[0:N_HEADS, :]
    q_pos = i * q_blk + lax.broadcasted_iota(I32, (1, q_blk), 1)
    key_lim = (q_pos // CHUNK + 1) * CHUNK
    row_iota = lax.broadcasted_iota(I32, (tk, q_blk), 0)

    def score_tile(t, _):
        off = pl.multiple_of(t * tk, tk)
        s = jnp.dot(kk_ref[pl.ds(off, tk), 0:LANES], qi_ext,
                    preferred_element_type=F32)
        acc = jnp.zeros((tk, q_blk), F32)
        for h in range(N_HEADS):
            acc = acc + w_rows[h:h + 1, :] * jnp.maximum(s[:, h * q_blk:(h + 1) * q_blk], 0.0)
        bits = pltpu.bitcast(acc, I32)
        key = bits ^ ((bits >> 31) & 0x7FFFFFFF)
        key = jnp.where(off + row_iota < key_lim, key, INT_MIN)
        keys_scr[pl.ds(off, tk), :] = key
        bmax_scr[...] = jnp.maximum(
            bmax_scr[...], jnp.max(key.reshape(tk // N_BUCKETS, N_BUCKETS, q_blk), axis=0))
        return 0

    bmax_scr[...] = jnp.full_like(bmax_scr, INT_MIN)
    lax.fori_loop(0, n_tiles, score_tile, 0)

    def count(pred):
        def tile(t, acc):
            off = pl.multiple_of(t * tk, tk)
            x = pred(keys_scr[pl.ds(off, tk), :], off + row_iota)
            return acc + jnp.sum(x.reshape(tk // 64, 64, q_blk), axis=0)
        acc = lax.fori_loop(0, n_tiles, tile, jnp.zeros((64, q_blk), I32))
        return _col_reduce(acc, jnp.sum)

    bmax = bmax_scr[...]
    base = _col_reduce(bmax, jnp.min)
    span = _col_reduce(bmax, jnp.max) - base
    n_bits = jnp.max(jnp.where(span < 0, 32, 32 - lax.clz(span)))

    def radix_pass(b, carry):
        prefix, cnt_ge = carry
        cand = prefix | jnp.left_shift(jnp.int32(1), n_bits - 1 - b)
        cand_key = base + cand
        cnt = count(lambda k, _: jnp.where(k >= cand_key, 1, 0))
        in_span = (cand ^ INT_MIN) <= (span ^ INT_MIN)
        ok = jnp.where(in_span, jnp.where(cnt >= topk, 1, 0), 0) == 1
        return jnp.where(ok, cand, prefix), jnp.where(ok, cnt, cnt_ge)

    cnt_base = count(lambda k, _: jnp.where(k >= base, 1, 0))
    prefix, cnt_ge = lax.fori_loop(0, n_bits, radix_pass, (jnp.zeros((1, q_blk), I32), cnt_base))
    kth = base + prefix
    thr = jnp.maximum(kth, MIN_VALID_KEY)

    has_tie = jnp.where(cnt_ge > topk, jnp.where(kth >= MIN_VALID_KEY, 1, 0), 0)

    @pl.when(jnp.max(has_tie) > 0)
    def _():
        n_gt = count(lambda k, _: jnp.where(k > thr, 1, 0))
        need = topk - n_gt

        def idx_pass(b, pre):
            cand = pre | jnp.left_shift(jnp.int32(1), idx_bits - 1 - b)
            c = count(lambda k, kidx: jnp.where(k == thr, jnp.where(kidx < cand, 1, 0), 0))
            return jnp.where(c < need, cand, pre)

        last = lax.fori_loop(0, idx_bits, idx_pass, jnp.zeros((1, q_blk), I32))

        def drop_tile(t, _):
            off = pl.multiple_of(t * tk, tk)
            k = keys_scr[pl.ds(off, tk), :]
            drop = jnp.where(k == thr, jnp.where(off + row_iota > last, INT_MIN, k), k)
            keys_scr[pl.ds(off, tk), :] = drop
            return 0

        lax.fori_loop(0, n_tiles, drop_tile, 0)

    acc_scr[...] = jnp.zeros_like(acc_scr)

    heads = [slice(h * q_blk, (h + 1) * q_blk) for h in range(N_HEADS)]

    def lane_row(x8):
        return jnp.concatenate([x8[h:h + 1, :] for h in range(N_HEADS)], axis=1)

    def tile_operands(t):
        off = pl.multiple_of(t * tk, tk)
        bias = jnp.where(keys_scr[pl.ds(off, tk), :] >= thr, 0.0, NEG_INF)
        return kk_ref[pl.ds(off, tk), :], bias, vat_ref[:, pl.ds(off, tk)]

    def attn_tile(t, m_all):
        kt, bias, vt = tile_operands(t)
        set_row_term(jnp.zeros((1, width), F32))
        logits = jnp.dot(kt, rhs_scr[...], preferred_element_type=F32)
        m_rows = []
        for h in range(N_HEADS):
            s = logits[:, heads[h]] + bias
            m_new = jnp.maximum(m_all[h:h + 1, :], _col_reduce(s, jnp.max))
            p_scr[:, heads[h]] = jnp.exp2(s - m_new).astype(BF16)
            m_rows.append(m_new)
        m_new = jnp.concatenate(m_rows, axis=0)
        pv = jnp.dot(vt, p_scr[...], preferred_element_type=F32)
        acc_scr[...] = acc_scr[...] * lane_row(jnp.exp2(m_all - m_new)) + pv
        return m_new

    def one_pass_tile(t, carry):
        m_ref, worst = carry
        kt, bias, vt = tile_operands(t)
        set_row_term(-lane_row(m_ref))
        x_all = jnp.dot(kt, rhs_scr[...], preferred_element_type=F32)
        tmax = []
        for h in range(N_HEADS):
            x = x_all[:, heads[h]] + bias
            p_scr[:, heads[h]] = jnp.exp2(x).astype(BF16)
            tmax.append(_col_reduce(x, jnp.max))
        tmax = jnp.concatenate(tmax, axis=0)
        up = jnp.maximum(tmax, 0.0)
        pv = jnp.dot(vt, p_scr[...], preferred_element_type=F32)
        acc_scr[...] = (acc_scr[...] + pv) * lane_row(jnp.exp2(-up))
        return m_ref + up, jnp.maximum(worst, tmax)

    m0 = jnp.full((N_HEADS, q_blk), M_INIT, F32)
    m1 = attn_tile(0, m0)
    _, worst = lax.fori_loop(1, n_tiles, one_pass_tile, (m1, m0))

    @pl.when(jnp.max(worst) > MAX_EXP_ARG)
    def _():
        acc_scr[...] = jnp.zeros_like(acc_scr)
        lax.fori_loop(0, n_tiles, attn_tile, m0)

    denom = acc_scr[LANE_ONES:LANE_ONES + 1, :]
    for pair in range(N_HEADS // 2):
        halves = []
        for h in (2 * pair, 2 * pair + 1):
            halves.append(acc_scr[0:HEAD_DIM, heads[h]] / denom[:, heads[h]])
        o_ref[:, pair * LANES:(pair + 1) * LANES] = (
            jnp.concatenate(halves, axis=0).T.astype(BF16))


def _dsa_call(qit, qat, auxt, kk, vat, *, q_blk, tk):
    B, _, _, S = qit.shape
    topk = min(TOPK_MAX, S // 4)
    idx_bits = max(1, int(np.ceil(np.log2(S))))
    width = N_HEADS * q_blk
    hm_spec = pl.BlockSpec((None, N_HEADS, HEAD_DIM, q_blk), lambda b, i: (b, 0, 0, i))
    return pl.pallas_call(
        functools.partial(_dsa_kernel, q_blk=q_blk, tk=tk, topk=topk, idx_bits=idx_bits),
        grid=(B, S // q_blk),
        in_specs=[
            hm_spec, hm_spec,
            pl.BlockSpec((None, 2 * N_HEADS, q_blk), lambda b, i: (b, 0, i)),
            pl.BlockSpec((None, S, 2 * LANES), lambda b, i: (b, 0, 0)),
            pl.BlockSpec((None, LANES, S), lambda b, i: (b, 0, 0)),
        ],
        out_specs=pl.BlockSpec((None, q_blk, W_HEADS), lambda b, i: (b, i, 0)),
        out_shape=jax.ShapeDtypeStruct((B, S, W_HEADS), BF16),
        scratch_shapes=[
            pltpu.VMEM((S, q_blk), I32),
            pltpu.VMEM((LANES, width), F32),
            pltpu.VMEM((tk, width), BF16),
            pltpu.VMEM((2 * LANES, width), BF16),
            pltpu.VMEM((N_BUCKETS, q_blk), I32),
        ],
        compiler_params=_params(2),
        name="dsa",
    )(qit, qat, auxt, kk, vat)


def _fox_kernel(qi_tbl, kj_tbl, qt_ref, k_ref, vt_ref, auxq_ref, auxk_ref, neg_ref, o_ref,
                m_scr, acc_scr, pv_scr, done_scr, *, t):
    step = pl.program_id(1)
    i = qi_tbl[step]
    j = kj_tbl[step]
    top = lax.broadcasted_iota(I32, (LANES, t), 0) < HEAD_DIM
    piece_row = lax.broadcasted_iota(I32, (PIECE_ROWS, t), 0)
    ones_rows = jnp.ones((PIECE_ROWS, t), BF16)

    @pl.when(j == 0)
    def _():
        m_scr[...] = jnp.full_like(m_scr, M_INIT)
        acc_scr[...] = jnp.zeros_like(acc_scr)

    def head_operands(h, r):
        pair, e = divmod(h, 2)
        qp = qt_ref[pair * LANES:(pair + 1) * LANES, :]
        qm = jnp.where(top, qp, 0) if e == 0 else jnp.where(top, 0, qp)
        hi = r.astype(BF16).astype(F32)
        mid = (r - hi).astype(BF16).astype(F32)
        lo = r - hi - mid
        pieces = jnp.where(piece_row == 0, hi, jnp.where(piece_row == 1, mid,
                                                         jnp.where(piece_row == 2, lo, 0.0)))
        rhs = jnp.concatenate([qm, pieces.astype(BF16), neg_ref[h]], axis=0)
        v_ext = jnp.concatenate([vt_ref[h * HEAD_DIM:(h + 1) * HEAD_DIM, :], ones_rows], axis=0)
        return k_ref[:, 2 * pair * LANES:(2 * pair + 2) * LANES], rhs, v_ext

    def two_pass_body():
        causal = (j * t + lax.broadcasted_iota(I32, (t, t), 0)
                  <= i * t + lax.broadcasted_iota(I32, (t, t), 1))
        for h in range(N_HEADS):
            lhs, rhs, v_ext = head_operands(h, auxq_ref[N_HEADS + h:N_HEADS + h + 1, :])
            s = jnp.dot(lhs, rhs, preferred_element_type=F32)
            s = jnp.where(causal, s, NEG_INF)
            m_prev = m_scr[h:h + 1, :]
            m_new = jnp.maximum(m_prev, _col_reduce(s, jnp.max))
            p = jnp.exp2(s - m_new).astype(BF16)
            m_scr[h:h + 1, :] = m_new
            acc_scr[h] = (acc_scr[h] * jnp.exp2(m_prev - m_new)
                          + jnp.dot(v_ext, p, preferred_element_type=F32))

    def one_pass_body(diag):
        if diag:
            causal = (lax.broadcasted_iota(I32, (t, t), 0)
                      <= lax.broadcasted_iota(I32, (t, t), 1))
        tmax, shifts = [], []
        for h in range(N_HEADS):
            cq = auxq_ref[N_HEADS + h:N_HEADS + h + 1, :]
            ck_first = auxk_ref[N_HEADS + h:N_HEADS + h + 1, 0:1]
            ck_last = auxk_ref[N_HEADS + h:N_HEADS + h + 1, t - 1:t]
            shift = jnp.minimum(cq - ck_last, 0.0) - jnp.minimum(cq - ck_first, 0.0)
            lhs, rhs, v_ext = head_operands(h, cq - (m_scr[h:h + 1, :] + shift))
            x = jnp.dot(lhs, rhs, preferred_element_type=F32)
            if diag:
                x = jnp.where(causal, x, NEG_INF)
            tmax.append(_col_reduce(x, jnp.max))
            shifts.append(shift)
            pv_scr[h] = jnp.dot(v_ext, jnp.exp2(x).astype(BF16), preferred_element_type=F32)
        tmax_all = jnp.concatenate(tmax, axis=0)
        ok = jnp.logical_and(jnp.max(tmax_all) <= MAX_EXP_ARG, jnp.min(tmax_all) >= -MAX_EXP_ARG)

        @pl.when(ok)
        def _():
            for h in range(N_HEADS):
                up = jnp.maximum(tmax[h], 0.0)
                m_scr[h:h + 1, :] = m_scr[h:h + 1, :] + shifts[h] + up
                acc_scr[h] = (acc_scr[h] * jnp.exp2(-shifts[h]) + pv_scr[h]) * jnp.exp2(-up)

        return jnp.where(ok, 1, 0)

    done_scr[0] = 0

    @pl.when(jnp.logical_and(j > 0, j < i))
    def _():
        done_scr[0] = one_pass_body(False)

    @pl.when(jnp.logical_and(j > 0, j == i))
    def _():
        done_scr[0] = one_pass_body(True)

    @pl.when(done_scr[0] == 0)
    def _():
        two_pass_body()

    @pl.when(j == i)
    def _():
        for pair in range(N_HEADS // 2):
            halves = [acc_scr[h, 0:HEAD_DIM, :] / acc_scr[h, HEAD_DIM:HEAD_DIM + 1, :]
                      for h in (2 * pair, 2 * pair + 1)]
            o_ref[:, pair * LANES:(pair + 1) * LANES] = (
                jnp.concatenate(halves, axis=0).T.astype(BF16))


def _fox_call(qt, k, vt, auxt, *, t):
    B, S, _ = k.shape
    n = S // t
    neg = np.zeros((N_HEADS, LANES - PIECE_ROWS, t), np.float32)
    for h in range(N_HEADS):
        neg[h, N_PIECES * h:N_PIECES * (h + 1), :] = -1.0
    qi_tbl = np.concatenate([np.full(i + 1, i, np.int32) for i in range(n)])
    kj_tbl = np.concatenate([np.arange(i + 1, dtype=np.int32) for i in range(n)])
    grid_spec = pltpu.PrefetchScalarGridSpec(
        num_scalar_prefetch=2,
        grid=(B, len(qi_tbl)),
        in_specs=[
            pl.BlockSpec((None, W_HEADS, t), lambda b, s, qi, kj: (b, 0, qi[s])),
            pl.BlockSpec((None, t, 2 * W_HEADS), lambda b, s, qi, kj: (b, kj[s], 0)),
            pl.BlockSpec((None, W_HEADS, t), lambda b, s, qi, kj: (b, 0, kj[s])),
            pl.BlockSpec((None, 2 * N_HEADS, t), lambda b, s, qi, kj: (b, 0, qi[s])),
            pl.BlockSpec((None, 2 * N_HEADS, t), lambda b, s, qi, kj: (b, 0, kj[s])),
            pl.BlockSpec((N_HEADS, LANES - PIECE_ROWS, t), lambda b, s, qi, kj: (0, 0, 0)),
        ],
        out_specs=pl.BlockSpec((None, t, W_HEADS), lambda b, s, qi, kj: (b, qi[s], 0)),
        scratch_shapes=[
            pltpu.VMEM((N_HEADS, t), F32),
            pltpu.VMEM((N_HEADS, HEAD_DIM + PIECE_ROWS, t), F32),
            pltpu.VMEM((N_HEADS, HEAD_DIM + PIECE_ROWS, t), F32),
            pltpu.SMEM((1,), I32),
        ],
    )
    return pl.pallas_call(
        functools.partial(_fox_kernel, t=t),
        grid_spec=grid_spec,
        out_shape=jax.ShapeDtypeStruct((B, S, W_HEADS), BF16),
        compiler_params=_params(2),
        name="fox",
    )(jnp.asarray(qi_tbl), jnp.asarray(kj_tbl), qt, k, vt, auxt, auxt, jnp.asarray(neg, BF16))


def _merge_kernel(oa_ref, ob_ref, ga_ref, gb_ref, x_ref, g1_ref, sc_ref, sh_ref, n2_ref,
                  wa_ref, wb_ref, wo_ref, x1_ref, h2_ref):
    ya = jnp.dot(oa_ref[...], wa_ref[...], preferred_element_type=F32)
    yb = jnp.dot(ob_ref[...], wb_ref[...], preferred_element_type=F32)
    mix = ga_ref[...].astype(F32) * ya + gb_ref[...].astype(F32) * yb
    y = jnp.dot(mix.astype(BF16), wo_ref[...], preferred_element_type=F32)
    x1 = x_ref[...] + g1_ref[...] * y
    x1_ref[...] = x1
    h2_ref[...] = _rms_mod(x1, n2_ref[...], sc_ref[...], sh_ref[...]).astype(BF16)


def _merge_call(oa, ob, ga, gb, x, g1, sc2, sh2, n2, wa, wb, wo, *, tm):
    B, S, _ = x.shape
    row = lambda b, s: (b, s, 0)
    const2 = lambda b, s: (0, 0)
    vec = pl.BlockSpec((None, 1, D_MODEL), lambda b, s: (b, 0, 0))
    return pl.pallas_call(
        _merge_kernel,
        grid=(B, S // tm),
        in_specs=[
            pl.BlockSpec((None, tm, W_HEADS), row),
            pl.BlockSpec((None, tm, W_HEADS), row),
            pl.BlockSpec((None, tm, D_MODEL), row),
            pl.BlockSpec((None, tm, D_MODEL), row),
            pl.BlockSpec((None, tm, D_MODEL), row),
            vec, vec, vec,
            pl.BlockSpec((1, D_MODEL), const2),
            pl.BlockSpec((W_HEADS, D_MODEL), const2),
            pl.BlockSpec((W_HEADS, D_MODEL), const2),
            pl.BlockSpec((D_MODEL, D_MODEL), const2),
        ],
        out_specs=[pl.BlockSpec((None, tm, D_MODEL), row),
                   pl.BlockSpec((None, tm, D_MODEL), row)],
        out_shape=[jax.ShapeDtypeStruct((B, S, D_MODEL), F32),
                   jax.ShapeDtypeStruct((B, S, D_MODEL), BF16)],
        compiler_params=_params(2),
        name="merge",
    )(oa, ob, ga, gb, x, g1, sc2, sh2, n2, wa, wb, wo)


def _ffn_kernel(h_ref, x_ref, g2_ref, wup_ref, cw_ref, cb_ref, wdn_ref, fg_ref, o_ref,
                carry_scr, *, tm, fc, final):
    s_idx = pl.program_id(1)

    @pl.when(s_idx == 0)
    def _():
        carry_scr[...] = jnp.zeros_like(carry_scr)

    hb = h_ref[...]
    row = lax.broadcasted_iota(I32, (SUBLANES, fc), 0)
    acc = jnp.zeros((tm, D_MODEL), F32)
    for cblk in range(D_FF // fc):
        halves = []
        for part in range(2):
            col = part * D_FF + cblk * fc
            u = jnp.dot(hb, wup_ref[:, col:col + fc], preferred_element_type=F32)
            prev = carry_scr[:, col:col + fc]
            r1 = pltpu.roll(u, 1, 0)
            r2 = pltpu.roll(u, 2, 0)
            head1 = jnp.where(row == 0, prev[7:8], r1[:SUBLANES])
            head2 = jnp.where(row == 0, prev[6:7], jnp.where(row == 1, prev[7:8], r2[:SUBLANES]))
            u1 = jnp.concatenate([head1, r1[SUBLANES:]], axis=0)
            u2 = jnp.concatenate([head2, r2[SUBLANES:]], axis=0)
            carry_scr[:, col:col + fc] = u[tm - SUBLANES:tm]
            cw = cw_ref[:, col:col + fc]
            halves.append(cb_ref[:, col:col + fc] + (cw[0:1] * u2 + cw[1:2] * u1 + cw[2:3] * u))
        act = (jax.nn.silu(halves[0]) * halves[1]).astype(BF16)
        acc = acc + jnp.dot(act, wdn_ref[cblk * fc:(cblk + 1) * fc, :],
                            preferred_element_type=F32)
    x2 = x_ref[...] + g2_ref[...] * acc
    if final:
        var = jnp.mean(x2 * x2, axis=-1, keepdims=True)
        x2 = x2 * lax.rsqrt(var + EPS) * fg_ref[...]
    o_ref[...] = x2


def _ffn_call(h2, x1, g2, wup, cw, cb, wdn, fg, *, tm, fc, final):
    B, S, _ = x1.shape
    row = lambda b, s: (b, s, 0)
    const2 = lambda b, s: (0, 0)
    return pl.pallas_call(
        functools.partial(_ffn_kernel, tm=tm, fc=fc, final=final),
        grid=(B, S // tm),
        in_specs=[
            pl.BlockSpec((None, tm, D_MODEL), row),
            pl.BlockSpec((None, tm, D_MODEL), row),
            pl.BlockSpec((None, 1, D_MODEL), lambda b, s: (b, 0, 0)),
            pl.BlockSpec((D_MODEL, 2 * D_FF), const2, pipeline_mode=pl.Buffered(1)),
            pl.BlockSpec((SUBLANES, 2 * D_FF), const2),
            pl.BlockSpec((1, 2 * D_FF), const2),
            pl.BlockSpec((D_FF, D_MODEL), const2, pipeline_mode=pl.Buffered(1)),
            pl.BlockSpec((1, D_MODEL), const2),
        ],
        out_specs=pl.BlockSpec((None, tm, D_MODEL), row),
        out_shape=jax.ShapeDtypeStruct((B, S, D_MODEL), F32),
        scratch_shapes=[pltpu.VMEM((SUBLANES, 2 * D_FF), F32)],
        compiler_params=_params(2),
        name="ffn",
    )(h2, x1, g2, wup, cw, cb, wdn, fg)


def _reorder_w_in(w):
    o = np.cumsum([0, W_HEADS, HEAD_DIM, HEAD_DIM, W_HEADS, HEAD_DIM, N_HEADS,
                   W_HEADS, W_HEADS, W_HEADS, N_HEADS, 2 * D_MODEL])
    seg = lambda k: w[:, o[k]:o[k + 1]]
    q_a, k_a, v_a, q_i, k_i, w_i, q_f, k_f, v_f, f_f, gates = [seg(k) for k in range(11)]
    pad = jnp.zeros((w.shape[0], N_COLS - int(o[-1])), w.dtype)
    return jnp.concatenate([q_a, q_i, q_f, k_f, v_f, gates, k_i, k_a, v_a, w_i, f_f, pad],
                           axis=1).astype(BF16)


def kernel(x, c, positions, mod_w, mod_b, norm1_g, norm2_g, w_in, forget_bias, w_branch_a,
           w_branch_b, w_out, w_up, conv_w, conv_b, w_down, final_g):
    B, S, _ = x.shape
    depth = mod_w.shape[0]
    tm = min(TOKEN_TILE, S)
    mod = _mod_call(c, mod_w, mod_b)[:, :, :B].reshape(depth, 6, B, 1, D_MODEL)
    pos = positions.astype(F32).reshape(B, S, 1)

    inv_freq = ROPE_THETA ** (-jnp.arange(0, ROT_DIM, 2, dtype=F32) / ROT_DIM)
    l64 = np.arange(LANES) % HEAD_DIM
    invf = jnp.where(l64 < ROT_DIM, inv_freq[l64 % (ROT_DIM // 2)], 0.0).reshape(1, LANES)

    place = np.zeros((N_PIECES, LANES, LANES), np.float32)
    for h in range(N_HEADS):
        for piece in range(N_PIECES):
            place[piece, LANE_FORGET + h, LANE_CK + N_PIECES * h + piece] = 1.0
    place = jnp.asarray(place, BF16)

    for l in range(depth):
        sh1, sc1, g1, sh2, sc2, g2 = [mod[l, k] for k in range(6)]
        fb = jnp.zeros((1, LANES), F32).at[0, LANE_FORGET:LANE_FORGET + N_HEADS].set(forget_bias[l])
        (qat, qit, qft, kf, vft, ga, gb, kk, vat, auxt) = _in_call(
            x, sh1, sc1, norm1_g[l].reshape(1, D_MODEL), pos, invf, fb, place,
            _reorder_w_in(w_in[l]), tm=tm)
        oa = _dsa_call(qit, qat, auxt, kk, vat, q_blk=DSA_Q_BLOCK, tk=min(DSA_KEY_TILE, S))
        ob = _fox_call(qft, kf, vft, auxt, t=tm)
        x1, h2 = _merge_call(oa, ob, ga, gb, x, g1, sc2, sh2, norm2_g[l].reshape(1, D_MODEL),
                             w_branch_a[l].astype(BF16), w_branch_b[l].astype(BF16),
                             w_out[l].astype(BF16), tm=tm)
        cw8 = jnp.zeros((SUBLANES, 2 * D_FF), F32).at[:conv_w.shape[1]].set(conv_w[l])
        x = _ffn_call(h2, x1, g2, w_up[l].astype(BF16), cw8, conv_b[l].reshape(1, 2 * D_FF),
                      w_down[l].astype(BF16), final_g.reshape(1, D_MODEL),
                      tm=tm, fc=FFN_CHUNK, final=(l == depth - 1))
    return x
```

```python
import functools

import numpy as np
import jax
import jax.numpy as jnp
from jax import lax
from jax.experimental import pallas as pl
from jax.experimental.pallas import tpu as pltpu

F32 = jnp.float32
BF16 = jnp.bfloat16
I32 = jnp.int32

D_MODEL = 1024
CHUNK = 64
HEAD_DIM = 64
N_HEADS = 8
TOPK_MAX = 256
ROPE_THETA = 500000.0
ROT_DIM = HEAD_DIM // 4
D_FF = 2816
EPS = 1e-6
NEG_INF = -1e30
W_HEADS = N_HEADS * HEAD_DIM

LANES = 128
SUBLANES = 8
VMEM_LIMIT = 56 * 1024 * 1024

TOKEN_TILE = 512
DSA_Q_BLOCK = 2 * LANES
DSA_KEY_TILE = 512
FFN_CHUNK = 256

COL_QA = 0
COL_QI = 512
COL_QF = 1024
COL_KF = 1536
COL_VF = 2048
COL_GATES = 2560
COL_SMALL = 4608
N_COLS = 4864
LANE_WIDX = 64
LANE_FORGET = 72
LANE_ONES = 80
N_PIECES = 3
PIECE_ROWS = 16
LANE_CK = PIECE_ROWS

INT_MIN = -(2 ** 31)
M_INIT = -3.0e38
LOG2E = 1.4426950408889634
MAX_EXP_ARG = 60.0


def _f32_key(v):
    b = int(np.float32(v).view(np.int32))
    return b ^ ((b >> 31) & 0x7FFFFFFF)


MIN_VALID_KEY = _f32_key(0.5 * NEG_INF) + 1


def _params(n_grid):
    return pltpu.CompilerParams(
        dimension_semantics=("arbitrary",) * n_grid, vmem_limit_bytes=VMEM_LIMIT)


def _col_reduce(x, op):
    rows = x.shape[0]
    part = op(x.reshape(rows // SUBLANES, SUBLANES, x.shape[1]), axis=0)
    return op(part, axis=0, keepdims=True)


def _mod_kernel(c_ref, w_ref, b_ref, o_ref):
    o_ref[...] = jnp.dot(c_ref[...], w_ref[...], precision=lax.Precision.HIGHEST,
                         preferred_element_type=F32) + b_ref[...]


def _mod_call(c, mod_w, mod_b):
    depth = mod_w.shape[0]
    c8 = jnp.zeros((SUBLANES, D_MODEL), F32).at[: c.shape[0]].set(c)
    b4 = mod_b.reshape(depth, 6, 1, D_MODEL)
    return pl.pallas_call(
        _mod_kernel,
        grid=(depth, 6),
        in_specs=[
            pl.BlockSpec((SUBLANES, D_MODEL), lambda l, j: (0, 0)),
            pl.BlockSpec((None, D_MODEL, D_MODEL), lambda l, j: (l, 0, j)),
            pl.BlockSpec((None, None, 1, D_MODEL), lambda l, j: (l, j, 0, 0)),
        ],
        out_specs=pl.BlockSpec((None, None, SUBLANES, D_MODEL), lambda l, j: (l, j, 0, 0)),
        out_shape=jax.ShapeDtypeStruct((depth, 6, SUBLANES, D_MODEL), F32),
        compiler_params=_params(2),
        name="mod",
    )(c8, mod_w, b4)


def _rms_mod(x, gain, scale, shift):
    var = jnp.mean(x * x, axis=-1, keepdims=True)
    y = x * lax.rsqrt(var + EPS) * gain
    return y * (1.0 + scale) + shift


def _split3(x):
    p0 = x.astype(BF16)
    r1 = x - p0.astype(F32)
    p1 = r1.astype(BF16)
    return p0, p1, (r1 - p1.astype(F32)).astype(BF16)


def _in_kernel(x_ref, sh_ref, sc_ref, g_ref, pos_ref, invf_ref, fb_ref, place_ref, w_ref,
               qat_ref, qit_ref, qft_ref, kf_ref, vft_ref, ga_ref, gb_ref,
               kk_ref, vat_ref, auxt_ref, carry_scr, *, tm):
    s_idx = pl.program_id(1)

    @pl.when(s_idx == 0)
    def _():
        carry_scr[...] = jnp.zeros_like(carry_scr)

    h = _rms_mod(x_ref[...], g_ref[...], sc_ref[...], sh_ref[...])
    hb = h.astype(BF16)

    ang = pos_ref[...] * invf_ref[...]
    cos = jnp.cos(ang)
    sin = jnp.sin(ang)
    l64 = lax.broadcasted_iota(I32, (tm, LANES), 1) & (HEAD_DIM - 1)
    half = ROT_DIM // 2
    t_cos = jnp.where(l64 < ROT_DIM, cos, 1.0)
    t_lo = jnp.where(l64 < half, -sin, 0.0)
    t_hi = jnp.where(l64 < half, 0.0, jnp.where(l64 < ROT_DIM, sin, 0.0))

    def rope(xc):
        return (xc * t_cos + pltpu.roll(xc, LANES - half, 1) * t_lo
                + pltpu.roll(xc, half, 1) * t_hi)

    def proj(col, width):
        return jnp.dot(hb, w_ref[:, col:col + width], preferred_element_type=F32)

    for col, out_ref, scale in ((COL_QA, qat_ref, 0.125 * LOG2E), (COL_QI, qit_ref, 0.125)):
        p = proj(col, W_HEADS)
        for cblk in range(W_HEADS // LANES):
            r = (rope(p[:, cblk * LANES:(cblk + 1) * LANES]) * scale).T.astype(BF16)
            out_ref[2 * cblk] = r[:HEAD_DIM]
            out_ref[2 * cblk + 1] = r[HEAD_DIM:]

    for col, out_ref, scale in ((COL_QF, qft_ref, 0.125 * LOG2E), (COL_VF, vft_ref, 1.0)):
        p = proj(col, W_HEADS)
        for cblk in range(W_HEADS // LANES):
            sl = slice(cblk * LANES, (cblk + 1) * LANES)
            out_ref[sl, :] = (p[:, sl] * scale).T.astype(BF16)
    ga_ref[...] = jax.nn.sigmoid(proj(COL_GATES, D_MODEL)).astype(BF16)
    gb_ref[...] = jax.nn.sigmoid(proj(COL_GATES + D_MODEL, D_MODEL)).astype(BF16)

    lane = lax.broadcasted_iota(I32, (tm, LANES), 1)
    ones_lanes = jnp.where(lane < N_PIECES, 1.0, 0.0)
    small = proj(COL_SMALL, 2 * LANES)
    kk_ref[...] = jnp.concatenate([rope(small[:, :LANES]), ones_lanes], axis=1).astype(BF16)
    blk = small[:, LANES:]

    is_f = (lane >= LANE_FORGET) & (lane < LANE_FORGET + N_HEADS)
    z = blk + fb_ref[...]
    logf = jnp.minimum(z, 0.0) - jnp.log1p(jnp.exp(-jnp.abs(z)))
    logf = jnp.where(is_f, logf, 0.0)
    tri = (lax.broadcasted_iota(I32, (tm, tm), 0)
           >= lax.broadcasted_iota(I32, (tm, tm), 1)).astype(BF16)
    p0, p1, p2 = _split3(logf)
    cum = (jnp.dot(tri, p0, preferred_element_type=F32)
           + jnp.dot(tri, p1, preferred_element_type=F32)
           + jnp.dot(tri, p2, preferred_element_type=F32)) + carry_scr[0:1, :]
    carry_scr[0:1, :] = cum[tm - 1:tm, :]
    cum = cum * LOG2E

    c0, c1, c2 = _split3(cum)
    kaux = (jnp.dot(c0, place_ref[0], preferred_element_type=F32)
            + jnp.dot(c1, place_ref[1], preferred_element_type=F32)
            + jnp.dot(c2, place_ref[2], preferred_element_type=F32) + ones_lanes).astype(BF16)
    kf = proj(COL_KF, W_HEADS).astype(BF16)
    for pair in range(N_HEADS // 2):
        kf_ref[:, 2 * pair * LANES:(2 * pair + 1) * LANES] = kf[:, pair * LANES:(pair + 1) * LANES]
        kf_ref[:, (2 * pair + 1) * LANES:(2 * pair + 2) * LANES] = kaux

    is_w = (lane >= LANE_WIDX) & (lane < LANE_WIDX + N_HEADS)
    comb = jnp.where(is_w, blk * (N_HEADS ** -0.5), jnp.where(is_f, cum, blk))
    comb_t = jnp.where(lane == LANE_ONES, 1.0, comb).T
    vat_ref[...] = comb_t.astype(BF16)
    auxt_ref[...] = comb_t[LANE_WIDX:LANE_WIDX + 2 * N_HEADS]


def _in_call(x, sh, sc, gain, pos, invf, fb, place, w_all, *, tm):
    B, S, _ = x.shape
    hm_t = jax.ShapeDtypeStruct((B, N_HEADS, HEAD_DIM, S), BF16)
    wide_t = jax.ShapeDtypeStruct((B, W_HEADS, S), BF16)
    gate = jax.ShapeDtypeStruct((B, S, D_MODEL), BF16)
    row = lambda b, s: (b, s, 0)
    col = lambda b, s: (b, 0, s)
    vec = pl.BlockSpec((None, 1, D_MODEL), lambda b, s: (b, 0, 0))
    const2 = lambda b, s: (0, 0)
    hm_spec = pl.BlockSpec((None, N_HEADS, HEAD_DIM, tm), lambda b, s: (b, 0, 0, s))
    return pl.pallas_call(
        functools.partial(_in_kernel, tm=tm),
        grid=(B, S // tm),
        in_specs=[
            pl.BlockSpec((None, tm, D_MODEL), row),
            vec, vec,
            pl.BlockSpec((1, D_MODEL), const2),
            pl.BlockSpec((None, tm, 1), row),
            pl.BlockSpec((1, LANES), const2),
            pl.BlockSpec((1, LANES), const2),
            pl.BlockSpec((N_PIECES, LANES, LANES), lambda b, s: (0, 0, 0)),
            pl.BlockSpec((D_MODEL, N_COLS), const2),
        ],
        out_specs=[
            hm_spec, hm_spec,
            pl.BlockSpec((None, W_HEADS, tm), col),
            pl.BlockSpec((None, tm, 2 * W_HEADS), row),
            pl.BlockSpec((None, W_HEADS, tm), col),
            pl.BlockSpec((None, tm, D_MODEL), row),
            pl.BlockSpec((None, tm, D_MODEL), row),
            pl.BlockSpec((None, tm, 2 * LANES), row),
            pl.BlockSpec((None, LANES, tm), col),
            pl.BlockSpec((None, 2 * N_HEADS, tm), col),
        ],
        out_shape=[hm_t, hm_t, wide_t,
                   jax.ShapeDtypeStruct((B, S, 2 * W_HEADS), BF16),
                   wide_t, gate, gate,
                   jax.ShapeDtypeStruct((B, S, 2 * LANES), BF16),
                   jax.ShapeDtypeStruct((B, LANES, S), BF16),
                   jax.ShapeDtypeStruct((B, 2 * N_HEADS, S), F32)],
        scratch_shapes=[pltpu.VMEM((SUBLANES, LANES), F32)],
        compiler_params=_params(2),
        name="in_proj",
    )(x, sh, sc, gain, pos, invf, fb, place, w_all)


def _dsa_kernel(qit_ref, qat_ref, aux_ref, kk_ref, vat_ref, o_ref,
                keys_scr, acc_scr, p_scr, rhs_scr, *, q_blk, tk, topk, idx_bits):
    i = pl.program_id(1)
    width = N_HEADS * q_blk
    n_tiles = ((i + 1) * q_blk + tk - 1) // tk

    zeros = jnp.zeros((HEAD_DIM, width), BF16)
    qi_ext = jnp.concatenate(
        [jnp.concatenate([qit_ref[h] for h in range(N_HEADS)], axis=1), zeros], axis=0)
    rhs_scr[...] = jnp.concatenate(
        [zeros, jnp.concatenate([qat_ref[h] for h in range(N_HEADS)], axis=1),
         jnp.zeros((LANES, width), BF16)], axis=0)
    piece_row = lax.broadcasted_iota(I32, (PIECE_ROWS, width), 0)

    def set_row_term(r):
        hi = r.astype(BF16).astype(F32)
        mid = (r - hi).astype(BF16).astype(F32)
        lo = r - hi - mid
        blk = jnp.where(piece_row == 0, hi, jnp.where(piece_row == 1, mid,
                                                      jnp.where(piece_row == 2, lo, 0.0)))
        rhs_scr[2 * HEAD_DIM:2 * HEAD_DIM + PIECE_ROWS, :] = blk.astype(BF16)
    w_rows = aux_ref[0:N_HEADS, :]
    q_pos = i * q_blk + lax.broadcasted_iota(I32, (1, q_blk), 1)
    key_lim = (q_pos // CHUNK + 1) * CHUNK
    row_iota = lax.broadcasted_iota(I32, (tk, q_blk), 0)

    def score_tile(t, _):
        off = pl.multiple_of(t * tk, tk)
        s = jnp.dot(kk_ref[pl.ds(off, tk), 0:LANES], qi_ext,
                    preferred_element_type=F32)
        acc = jnp.zeros((tk, q_blk), F32)
        for h in range(N_HEADS):
            acc = acc + w_rows[h:h + 1, :] * jnp.maximum(s[:, h * q_blk:(h + 1) * q_blk], 0.0)
        bits = pltpu.bitcast(acc, I32)
        key = bits ^ ((bits >> 31) & 0x7FFFFFFF)
        keys_scr[pl.ds(off, tk), :] = jnp.where(off + row_iota < key_lim, key, INT_MIN)
        return 0

    lax.fori_loop(0, n_tiles, score_tile, 0)

    def count(pred):
        def tile(t, acc):
            off = pl.multiple_of(t * tk, tk)
            x = pred(keys_scr[pl.ds(off, tk), :], off + row_iota)
            return acc + jnp.sum(x.reshape(tk // 64, 64, q_blk), axis=0)
        acc = lax.fori_loop(0, n_tiles, tile, jnp.zeros((64, q_blk), I32))
        return _col_reduce(acc, jnp.sum)

    def radix_pass(b, carry):
        prefix, cnt_ge = carry
        cand = prefix | jnp.left_shift(jnp.int32(1), 31 - b)
        cand_s = cand ^ INT_MIN
        cnt = count(lambda k, _: jnp.where(k >= cand_s, 1, 0))
        ok = cnt >= topk
        return jnp.where(ok, cand, prefix), jnp.where(ok, cnt, cnt_ge)

    total = jnp.zeros((1, q_blk), I32) + n_tiles * tk
    prefix, cnt_ge = lax.fori_loop(0, 32, radix_pass, (jnp.zeros((1, q_blk), I32), total))
    kth = prefix ^ INT_MIN
    thr = jnp.maximum(kth, MIN_VALID_KEY)

    has_tie = jnp.where(cnt_ge > topk, jnp.where(kth >= MIN_VALID_KEY, 1, 0), 0)

    @pl.when(jnp.max(has_tie) > 0)
    def _():
        n_gt = count(lambda k, _: jnp.where(k > thr, 1, 0))
        need = topk - n_gt

        def idx_pass(b, pre):
            cand = pre | jnp.left_shift(jnp.int32(1), idx_bits - 1 - b)
            c = count(lambda k, kidx: jnp.where(k == thr, jnp.where(kidx < cand, 1, 0), 0))
            return jnp.where(c < need, cand, pre)

        last = lax.fori_loop(0, idx_bits, idx_pass, jnp.zeros((1, q_blk), I32))

        def drop_tile(t, _):
            off = pl.multiple_of(t * tk, tk)
            k = keys_scr[pl.ds(off, tk), :]
            drop = jnp.where(k == thr, jnp.where(off + row_iota > last, INT_MIN, k), k)
            keys_scr[pl.ds(off, tk), :] = drop
            return 0

        lax.fori_loop(0, n_tiles, drop_tile, 0)

    acc_scr[...] = jnp.zeros_like(acc_scr)

    heads = [slice(h * q_blk, (h + 1) * q_blk) for h in range(N_HEADS)]

    def lane_row(x8):
        return jnp.concatenate([x8[h:h + 1, :] for h in range(N_HEADS)], axis=1)

    def tile_operands(t):
        off = pl.multiple_of(t * tk, tk)
        bias = jnp.where(keys_scr[pl.ds(off, tk), :] >= thr, 0.0, NEG_INF)
        return kk_ref[pl.ds(off, tk), :], bias, vat_ref[:, pl.ds(off, tk)]

    def attn_tile(t, m_all):
        kt, bias, vt = tile_operands(t)
        set_row_term(jnp.zeros((1, width), F32))
        logits = jnp.dot(kt, rhs_scr[...], preferred_element_type=F32)
        m_rows = []
        for h in range(N_HEADS):
            s = logits[:, heads[h]] + bias
            m_new = jnp.maximum(m_all[h:h + 1, :], _col_reduce(s, jnp.max))
            p_scr[:, heads[h]] = jnp.exp2(s - m_new).astype(BF16)
            m_rows.append(m_new)
        m_new = jnp.concatenate(m_rows, axis=0)
        pv = jnp.dot(vt, p_scr[...], preferred_element_type=F32)
        acc_scr[...] = acc_scr[...] * lane_row(jnp.exp2(m_all - m_new)) + pv
        return m_new

    def one_pass_tile(t, carry):
        m_ref, worst = carry
        kt, bias, vt = tile_operands(t)
        set_row_term(-lane_row(m_ref))
        x_all = jnp.dot(kt, rhs_scr[...], preferred_element_type=F32)
        tmax = []
        for h in range(N_HEADS):
            x = x_all[:, heads[h]] + bias
            p_scr[:, heads[h]] = jnp.exp2(x).astype(BF16)
            tmax.append(_col_reduce(x, jnp.max))
        tmax = jnp.concatenate(tmax, axis=0)
        up = jnp.maximum(tmax, 0.0)
        pv = jnp.dot(vt, p_scr[...], preferred_element_type=F32)
        acc_scr[...] = (acc_scr[...] + pv) * lane_row(jnp.exp2(-up))
        return m_ref + up, jnp.maximum(worst, tmax)

    m0 = jnp.full((N_HEADS, q_blk), M_INIT, F32)
    m1 = attn_tile(0, m0)
    _, worst = lax.fori_loop(1, n_tiles, one_pass_tile, (m1, m0))

    @pl.when(jnp.max(worst) > MAX_EXP_ARG)
    def _():
        acc_scr[...] = jnp.zeros_like(acc_scr)
        lax.fori_loop(0, n_tiles, attn_tile, m0)

    denom = acc_scr[LANE_ONES:LANE_ONES + 1, :]
    for pair in range(N_HEADS // 2):
        halves = []
        for h in (2 * pair, 2 * pair + 1):
            halves.append(acc_scr[0:HEAD_DIM, heads[h]] / denom[:, heads[h]])
        o_ref[:, pair * LANES:(pair + 1) * LANES] = (
            jnp.concatenate(halves, axis=0).T.astype(BF16))


def _dsa_call(qit, qat, auxt, kk, vat, *, q_blk, tk):
    B, _, _, S = qit.shape
    topk = min(TOPK_MAX, S // 4)
    idx_bits = max(1, int(np.ceil(np.log2(S))))
    width = N_HEADS * q_blk
    hm_spec = pl.BlockSpec((None, N_HEADS, HEAD_DIM, q_blk), lambda b, i: (b, 0, 0, i))
    return pl.pallas_call(
        functools.partial(_dsa_kernel, q_blk=q_blk, tk=tk, topk=topk, idx_bits=idx_bits),
        grid=(B, S // q_blk),
        in_specs=[
            hm_spec, hm_spec,
            pl.BlockSpec((None, 2 * N_HEADS, q_blk), lambda b, i: (b, 0, i)),
            pl.BlockSpec((None, S, 2 * LANES), lambda b, i: (b, 0, 0)),
            pl.BlockSpec((None, LANES, S), lambda b, i: (b, 0, 0)),
        ],
        out_specs=pl.BlockSpec((None, q_blk, W_HEADS), lambda b, i: (b, i, 0)),
        out_shape=jax.ShapeDtypeStruct((B, S, W_HEADS), BF16),
        scratch_shapes=[
            pltpu.VMEM((S, q_blk), I32),
            pltpu.VMEM((LANES, width), F32),
            pltpu.VMEM((tk, width), BF16),
            pltpu.VMEM((2 * LANES, width), BF16),
        ],
        compiler_params=_params(2),
        name="dsa",
    )(qit, qat, auxt, kk, vat)


def _fox_kernel(qi_tbl, kj_tbl, qt_ref, k_ref, vt_ref, auxq_ref, auxk_ref, neg_ref, o_ref,
                m_scr, acc_scr, pv_scr, done_scr, *, t):
    step = pl.program_id(1)
    i = qi_tbl[step]
    j = kj_tbl[step]
    top = lax.broadcasted_iota(I32, (LANES, t), 0) < HEAD_DIM
    piece_row = lax.broadcasted_iota(I32, (PIECE_ROWS, t), 0)
    ones_rows = jnp.ones((PIECE_ROWS, t), BF16)

    @pl.when(j == 0)
    def _():
        m_scr[...] = jnp.minimum(auxq_ref[N_HEADS:2 * N_HEADS, :]
                                 - auxk_ref[N_HEADS:2 * N_HEADS, t - 1:t], 0.0)
        acc_scr[...] = jnp.zeros_like(acc_scr)

    def head_operands(h, r):
        pair, e = divmod(h, 2)
        qp = qt_ref[pair * LANES:(pair + 1) * LANES, :]
        qm = jnp.where(top, qp, 0) if e == 0 else jnp.where(top, 0, qp)
        hi = r.astype(BF16).astype(F32)
        mid = (r - hi).astype(BF16).astype(F32)
        lo = r - hi - mid
        pieces = jnp.where(piece_row == 0, hi, jnp.where(piece_row == 1, mid,
                                                         jnp.where(piece_row == 2, lo, 0.0)))
        rhs = jnp.concatenate([qm, pieces.astype(BF16), neg_ref[h]], axis=0)
        v_ext = jnp.concatenate([vt_ref[h * HEAD_DIM:(h + 1) * HEAD_DIM, :], ones_rows], axis=0)
        return k_ref[:, 2 * pair * LANES:(2 * pair + 2) * LANES], rhs, v_ext

    def two_pass_body():
        causal = (j * t + lax.broadcasted_iota(I32, (t, t), 0)
                  <= i * t + lax.broadcasted_iota(I32, (t, t), 1))
        for h in range(N_HEADS):
            lhs, rhs, v_ext = head_operands(h, auxq_ref[N_HEADS + h:N_HEADS + h + 1, :])
            s = jnp.dot(lhs, rhs, preferred_element_type=F32)
            s = jnp.where(causal, s, NEG_INF)
            m_prev = jnp.where(j == 0, M_INIT, m_scr[h:h + 1, :])
            m_new = jnp.maximum(m_prev, _col_reduce(s, jnp.max))
            p = jnp.exp2(s - m_new).astype(BF16)
            m_scr[h:h + 1, :] = m_new
            acc_scr[h] = (acc_scr[h] * jnp.exp2(m_prev - m_new)
                          + jnp.dot(v_ext, p, preferred_element_type=F32))

    def one_pass_body(diag):
        if diag:
            causal = (lax.broadcasted_iota(I32, (t, t), 0)
                      <= lax.broadcasted_iota(I32, (t, t), 1))
        tmax, shifts = [], []
        for h in range(N_HEADS):
            cq = auxq_ref[N_HEADS + h:N_HEADS + h + 1, :]
            ck_first = auxk_ref[N_HEADS + h:N_HEADS + h + 1, 0:1]
            ck_last = auxk_ref[N_HEADS + h:N_HEADS + h + 1, t - 1:t]
            shift = jnp.minimum(cq - ck_last, 0.0) - jnp.minimum(cq - ck_first, 0.0)
            shift = jnp.where(j == 0, 0.0, shift)
            lhs, rhs, v_ext = head_operands(h, cq - (m_scr[h:h + 1, :] + shift))
            x = jnp.dot(lhs, rhs, preferred_element_type=F32)
            if diag:
                x = jnp.where(causal, x, NEG_INF)
            tmax.append(_col_reduce(x, jnp.max))
            shifts.append(shift)
            pv_scr[h] = jnp.dot(v_ext, jnp.exp2(x).astype(BF16), preferred_element_type=F32)
        tmax_all = jnp.concatenate(tmax, axis=0)
        ok = jnp.logical_and(jnp.max(tmax_all) <= MAX_EXP_ARG, jnp.min(tmax_all) >= -MAX_EXP_ARG)

        @pl.when(ok)
        def _():
            for h in range(N_HEADS):
                up = jnp.maximum(tmax[h], 0.0)
                m_scr[h:h + 1, :] = m_scr[h:h + 1, :] + shifts[h] + up
                acc_scr[h] = (acc_scr[h] * jnp.exp2(-shifts[h]) + pv_scr[h]) * jnp.exp2(-up)

        return jnp.where(ok, 1, 0)

    done_scr[0] = 0

    @pl.when(j < i)
    def _():
        done_scr[0] = one_pass_body(False)

    @pl.when(j == i)
    def _():
        done_scr[0] = one_pass_body(True)

    @pl.when(done_scr[0] == 0)
    def _():
        two_pass_body()

    @pl.when(j == i)
    def _():
        for pair in range(N_HEADS // 2):
            halves = [acc_scr[h, 0:HEAD_DIM, :] / acc_scr[h, HEAD_DIM:HEAD_DIM + 1, :]
                      for h in (2 * pair, 2 * pair + 1)]
            o_ref[:, pair * LANES:(pair + 1) * LANES] = (
                jnp.concatenate(halves, axis=0).T.astype(BF16))


def _fox_call(qt, k, vt, auxt, *, t):
    B, S, _ = k.shape
    n = S // t
    neg = np.zeros((N_HEADS, LANES - PIECE_ROWS, t), np.float32)
    for h in range(N_HEADS):
        neg[h, N_PIECES * h:N_PIECES * (h + 1), :] = -1.0
    qi_tbl = np.concatenate([np.full(i + 1, i, np.int32) for i in range(n)])
    kj_tbl = np.concatenate([np.arange(i + 1, dtype=np.int32) for i in range(n)])
    grid_spec = pltpu.PrefetchScalarGridSpec(
        num_scalar_prefetch=2,
        grid=(B, len(qi_tbl)),
        in_specs=[
            pl.BlockSpec((None, W_HEADS, t), lambda b, s, qi, kj: (b, 0, qi[s])),
            pl.BlockSpec((None, t, 2 * W_HEADS), lambda b, s, qi, kj: (b, kj[s], 0)),
            pl.BlockSpec((None, W_HEADS, t), lambda b, s, qi, kj: (b, 0, kj[s])),
            pl.BlockSpec((None, 2 * N_HEADS, t), lambda b, s, qi, kj: (b, 0, qi[s])),
            pl.BlockSpec((None, 2 * N_HEADS, t), lambda b, s, qi, kj: (b, 0, kj[s])),
            pl.BlockSpec((N_HEADS, LANES - PIECE_ROWS, t), lambda b, s, qi, kj: (0, 0, 0)),
        ],
        out_specs=pl.BlockSpec((None, t, W_HEADS), lambda b, s, qi, kj: (b, qi[s], 0)),
        scratch_shapes=[
            pltpu.VMEM((N_HEADS, t), F32),
            pltpu.VMEM((N_HEADS, HEAD_DIM + PIECE_ROWS, t), F32),
            pltpu.VMEM((N_HEADS, HEAD_DIM + PIECE_ROWS, t), F32),
            pltpu.SMEM((1,), I32),
        ],
    )
    return pl.pallas_call(
        functools.partial(_fox_kernel, t=t),
        grid_spec=grid_spec,
        out_shape=jax.ShapeDtypeStruct((B, S, W_HEADS), BF16),
        compiler_params=_params(2),
        name="fox",
    )(jnp.asarray(qi_tbl), jnp.asarray(kj_tbl), qt, k, vt, auxt, auxt, jnp.asarray(neg, BF16))


def _merge_kernel(oa_ref, ob_ref, ga_ref, gb_ref, x_ref, g1_ref, sc_ref, sh_ref, n2_ref,
                  wa_ref, wb_ref, wo_ref, x1_ref, h2_ref):
    ya = jnp.dot(oa_ref[...], wa_ref[...], preferred_element_type=F32)
    yb = jnp.dot(ob_ref[...], wb_ref[...], preferred_element_type=F32)
    mix = ga_ref[...].astype(F32) * ya + gb_ref[...].astype(F32) * yb
    y = jnp.dot(mix.astype(BF16), wo_ref[...], preferred_element_type=F32)
    x1 = x_ref[...] + g1_ref[...] * y
    x1_ref[...] = x1
    h2_ref[...] = _rms_mod(x1, n2_ref[...], sc_ref[...], sh_ref[...]).astype(BF16)


def _merge_call(oa, ob, ga, gb, x, g1, sc2, sh2, n2, wa, wb, wo, *, tm):
    B, S, _ = x.shape
    row = lambda b, s: (b, s, 0)
    const2 = lambda b, s: (0, 0)
    vec = pl.BlockSpec((None, 1, D_MODEL), lambda b, s: (b, 0, 0))
    return pl.pallas_call(
        _merge_kernel,
        grid=(B, S // tm),
        in_specs=[
            pl.BlockSpec((None, tm, W_HEADS), row),
            pl.BlockSpec((None, tm, W_HEADS), row),
            pl.BlockSpec((None, tm, D_MODEL), row),
            pl.BlockSpec((None, tm, D_MODEL), row),
            pl.BlockSpec((None, tm, D_MODEL), row),
            vec, vec, vec,
            pl.BlockSpec((1, D_MODEL), const2),
            pl.BlockSpec((W_HEADS, D_MODEL), const2),
            pl.BlockSpec((W_HEADS, D_MODEL), const2),
            pl.BlockSpec((D_MODEL, D_MODEL), const2),
        ],
        out_specs=[pl.BlockSpec((None, tm, D_MODEL), row),
                   pl.BlockSpec((None, tm, D_MODEL), row)],
        out_shape=[jax.ShapeDtypeStruct((B, S, D_MODEL), F32),
                   jax.ShapeDtypeStruct((B, S, D_MODEL), BF16)],
        compiler_params=_params(2),
        name="merge",
    )(oa, ob, ga, gb, x, g1, sc2, sh2, n2, wa, wb, wo)


def _ffn_kernel(h_ref, x_ref, g2_ref, wup_ref, cw_ref, cb_ref, wdn_ref, fg_ref, o_ref,
                carry_scr, *, tm, fc, final):
    s_idx = pl.program_id(1)

    @pl.when(s_idx == 0)
    def _():
        carry_scr[...] = jnp.zeros_like(carry_scr)

    hb = h_ref[...]
    row = lax.broadcasted_iota(I32, (SUBLANES, fc), 0)
    acc = jnp.zeros((tm, D_MODEL), F32)
    for cblk in range(D_FF // fc):
        halves = []
        for part in range(2):
            col = part * D_FF + cblk * fc
            u = jnp.dot(hb, wup_ref[:, col:col + fc], preferred_element_type=F32)
            prev = carry_scr[:, col:col + fc]
            r1 = pltpu.roll(u, 1, 0)
            r2 = pltpu.roll(u, 2, 0)
            head1 = jnp.where(row == 0, prev[7:8], r1[:SUBLANES])
            head2 = jnp.where(row == 0, prev[6:7], jnp.where(row == 1, prev[7:8], r2[:SUBLANES]))
            u1 = jnp.concatenate([head1, r1[SUBLANES:]], axis=0)
            u2 = jnp.concatenate([head2, r2[SUBLANES:]], axis=0)
            carry_scr[:, col:col + fc] = u[tm - SUBLANES:tm]
            cw = cw_ref[:, col:col + fc]
            halves.append(cb_ref[:, col:col + fc] + (cw[0:1] * u2 + cw[1:2] * u1 + cw[2:3] * u))
        act = (jax.nn.silu(halves[0]) * halves[1]).astype(BF16)
        acc = acc + jnp.dot(act, wdn_ref[cblk * fc:(cblk + 1) * fc, :],
                            preferred_element_type=F32)
    x2 = x_ref[...] + g2_ref[...] * acc
    if final:
        var = jnp.mean(x2 * x2, axis=-1, keepdims=True)
        x2 = x2 * lax.rsqrt(var + EPS) * fg_ref[...]
    o_ref[...] = x2


def _ffn_call(h2, x1, g2, wup, cw, cb, wdn, fg, *, tm, fc, final):
    B, S, _ = x1.shape
    row = lambda b, s: (b, s, 0)
    const2 = lambda b, s: (0, 0)
    return pl.pallas_call(
        functools.partial(_ffn_kernel, tm=tm, fc=fc, final=final),
        grid=(B, S // tm),
        in_specs=[
            pl.BlockSpec((None, tm, D_MODEL), row),
            pl.BlockSpec((None, tm, D_MODEL), row),
            pl.BlockSpec((None, 1, D_MODEL), lambda b, s: (b, 0, 0)),
            pl.BlockSpec((D_MODEL, 2 * D_FF), const2, pipeline_mode=pl.Buffered(1)),
            pl.BlockSpec((SUBLANES, 2 * D_FF), const2),
            pl.BlockSpec((1, 2 * D_FF), const2),
            pl.BlockSpec((D_FF, D_MODEL), const2, pipeline_mode=pl.Buffered(1)),
            pl.BlockSpec((1, D_MODEL), const2),
        ],
        out_specs=pl.BlockSpec((None, tm, D_MODEL), row),
        out_shape=jax.ShapeDtypeStruct((B, S, D_MODEL), F32),
        scratch_shapes=[pltpu.VMEM((SUBLANES, 2 * D_FF), F32)],
        compiler_params=_params(2),
        name="ffn",
    )(h2, x1, g2, wup, cw, cb, wdn, fg)


def _reorder_w_in(w):
    o = np.cumsum([0, W_HEADS, HEAD_DIM, HEAD_DIM, W_HEADS, HEAD_DIM, N_HEADS,
                   W_HEADS, W_HEADS, W_HEADS, N_HEADS, 2 * D_MODEL])
    seg = lambda k: w[:, o[k]:o[k + 1]]
    q_a, k_a, v_a, q_i, k_i, w_i, q_f, k_f, v_f, f_f, gates = [seg(k) for k in range(11)]
    pad = jnp.zeros((w.shape[0], N_COLS - int(o[-1])), w.dtype)
    return jnp.concatenate([q_a, q_i, q_f, k_f, v_f, gates, k_i, k_a, v_a, w_i, f_f, pad],
                           axis=1).astype(BF16)


def kernel(x, c, positions, mod_w, mod_b, norm1_g, norm2_g, w_in, forget_bias, w_branch_a,
           w_branch_b, w_out, w_up, conv_w, conv_b, w_down, final_g):
    B, S, _ = x.shape
    depth = mod_w.shape[0]
    tm = min(TOKEN_TILE, S)
    mod = _mod_call(c, mod_w, mod_b)[:, :, :B].reshape(depth, 6, B, 1, D_MODEL)
    pos = positions.astype(F32).reshape(B, S, 1)

    inv_freq = ROPE_THETA ** (-jnp.arange(0, ROT_DIM, 2, dtype=F32) / ROT_DIM)
    l64 = np.arange(LANES) % HEAD_DIM
    invf = jnp.where(l64 < ROT_DIM, inv_freq[l64 % (ROT_DIM // 2)], 0.0).reshape(1, LANES)

    place = np.zeros((N_PIECES, LANES, LANES), np.float32)
    for h in range(N_HEADS):
        for piece in range(N_PIECES):
            place[piece, LANE_FORGET + h, LANE_CK + N_PIECES * h + piece] = 1.0
    place = jnp.asarray(place, BF16)

    for l in range(depth):
        sh1, sc1, g1, sh2, sc2, g2 = [mod[l, k] for k in range(6)]
        fb = jnp.zeros((1, LANES), F32).at[0, LANE_FORGET:LANE_FORGET + N_HEADS].set(forget_bias[l])
        (qat, qit, qft, kf, vft, ga, gb, kk, vat, auxt) = _in_call(
            x, sh1, sc1, norm1_g[l].reshape(1, D_MODEL), pos, invf, fb, place,
            _reorder_w_in(w_in[l]), tm=tm)
        oa = _dsa_call(qit, qat, auxt, kk, vat, q_blk=DSA_Q_BLOCK, tk=min(DSA_KEY_TILE, S))
        ob = _fox_call(qft, kf, vft, auxt, t=tm)
        x1, h2 = _merge_call(oa, ob, ga, gb, x, g1, sc2, sh2, norm2_g[l].reshape(1, D_MODEL),
                             w_branch_a[l].astype(BF16), w_branch_b[l].astype(BF16),
                             w_out[l].astype(BF16), tm=tm)
        cw8 = jnp.zeros((SUBLANES, 2 * D_FF), F32).at[:conv_w.shape[1]].set(conv_w[l])
        x = _ffn_call(h2, x1, g2, w_up[l].astype(BF16), cw8, conv_b[l].reshape(1, 2 * D_FF),
                      w_down[l].astype(BF16), final_g.reshape(1, D_MODEL),
                      tm=tm, fc=FFN_CHUNK, final=(l == depth - 1))
    return x
```

```python
import functools

import numpy as np
import jax
import jax.numpy as jnp
from jax import lax
from jax.experimental import pallas as pl
from jax.experimental.pallas import tpu as pltpu

F32 = jnp.float32
BF16 = jnp.bfloat16
I32 = jnp.int32

D_MODEL = 1024
CHUNK = 64
HEAD_DIM = 64
N_HEADS = 8
TOPK_MAX = 256
ROPE_THETA = 500000.0
ROT_DIM = HEAD_DIM // 4
D_FF = 2816
EPS = 1e-6
NEG_INF = -1e30
W_HEADS = N_HEADS * HEAD_DIM

LANES = 128
SUBLANES = 8
VMEM_LIMIT = 56 * 1024 * 1024

TOKEN_TILE = 512
DSA_Q_BLOCK = 2 * LANES
DSA_KEY_TILE = 512
FFN_CHUNK = 2816

COL_QA = 0
COL_QI = 512
COL_QF = 1024
COL_KF = 1536
COL_VF = 2048
COL_GATES = 2560
COL_SMALL = 4608
N_COLS = 4864
LANE_WIDX = 64
LANE_FORGET = 72
LANE_ONES = 80
N_PIECES = 3
PIECE_ROWS = 16
LANE_CK = PIECE_ROWS

INT_MIN = -(2 ** 31)
M_INIT = -3.0e38
LOG2E = 1.4426950408889634
MAX_EXP_ARG = 100.0


def _f32_key(v):
    b = int(np.float32(v).view(np.int32))
    return b ^ ((b >> 31) & 0x7FFFFFFF)


MIN_VALID_KEY = _f32_key(0.5 * NEG_INF) + 1


def _params(n_grid):
    return pltpu.CompilerParams(
        dimension_semantics=("arbitrary",) * n_grid, vmem_limit_bytes=VMEM_LIMIT)


def _col_reduce(x, op):
    rows = x.shape[0]
    part = op(x.reshape(rows // SUBLANES, SUBLANES, x.shape[1]), axis=0)
    return op(part, axis=0, keepdims=True)


def _mod_kernel(c_ref, w_ref, b_ref, o_ref):
    o_ref[...] = jnp.dot(c_ref[...], w_ref[...], precision=lax.Precision.HIGHEST,
                         preferred_element_type=F32) + b_ref[...]


def _mod_call(c, mod_w, mod_b):
    depth = mod_w.shape[0]
    c8 = jnp.zeros((SUBLANES, D_MODEL), F32).at[: c.shape[0]].set(c)
    b4 = mod_b.reshape(depth, 6, 1, D_MODEL)
    return pl.pallas_call(
        _mod_kernel,
        grid=(depth, 6),
        in_specs=[
            pl.BlockSpec((SUBLANES, D_MODEL), lambda l, j: (0, 0)),
            pl.BlockSpec((None, D_MODEL, D_MODEL), lambda l, j: (l, 0, j)),
            pl.BlockSpec((None, None, 1, D_MODEL), lambda l, j: (l, j, 0, 0)),
        ],
        out_specs=pl.BlockSpec((None, None, SUBLANES, D_MODEL), lambda l, j: (l, j, 0, 0)),
        out_shape=jax.ShapeDtypeStruct((depth, 6, SUBLANES, D_MODEL), F32),
        compiler_params=_params(2),
        name="mod",
    )(c8, mod_w, b4)


def _rms_mod(x, gain, scale, shift):
    var = jnp.mean(x * x, axis=-1, keepdims=True)
    y = x * lax.rsqrt(var + EPS) * gain
    return y * (1.0 + scale) + shift


def _split3(x):
    p0 = x.astype(BF16)
    r1 = x - p0.astype(F32)
    p1 = r1.astype(BF16)
    return p0, p1, (r1 - p1.astype(F32)).astype(BF16)


def _in_kernel(x_ref, sh_ref, sc_ref, g_ref, pos_ref, invf_ref, fb_ref, place_ref, w_ref,
               qat_ref, qit_ref, qft_ref, kf_ref, vft_ref, ga_ref, gb_ref,
               kk_ref, vat_ref, auxt_ref, carry_scr, *, tm):
    s_idx = pl.program_id(1)

    @pl.when(s_idx == 0)
    def _():
        carry_scr[...] = jnp.zeros_like(carry_scr)

    h = _rms_mod(x_ref[...], g_ref[...], sc_ref[...], sh_ref[...])
    hb = h.astype(BF16)

    ang = pos_ref[...] * invf_ref[...]
    cos = jnp.cos(ang)
    sin = jnp.sin(ang)
    l64 = lax.broadcasted_iota(I32, (tm, LANES), 1) & (HEAD_DIM - 1)
    half = ROT_DIM // 2
    t_cos = jnp.where(l64 < ROT_DIM, cos, 1.0)
    t_lo = jnp.where(l64 < half, -sin, 0.0)
    t_hi = jnp.where(l64 < half, 0.0, jnp.where(l64 < ROT_DIM, sin, 0.0))

    def rope(xc):
        return (xc * t_cos + pltpu.roll(xc, LANES - half, 1) * t_lo
                + pltpu.roll(xc, half, 1) * t_hi)

    def proj(col, width):
        return jnp.dot(hb, w_ref[:, col:col + width], preferred_element_type=F32)

    for col, out_ref, scale in ((COL_QA, qat_ref, 0.125 * LOG2E), (COL_QI, qit_ref, 0.125)):
        p = proj(col, W_HEADS)
        for cblk in range(W_HEADS // LANES):
            r = (rope(p[:, cblk * LANES:(cblk + 1) * LANES]) * scale).T.astype(BF16)
            out_ref[2 * cblk] = r[:HEAD_DIM]
            out_ref[2 * cblk + 1] = r[HEAD_DIM:]

    for col, out_ref, scale in ((COL_QF, qft_ref, 0.125 * LOG2E), (COL_VF, vft_ref, 1.0)):
        p = proj(col, W_HEADS)
        for cblk in range(W_HEADS // LANES):
            sl = slice(cblk * LANES, (cblk + 1) * LANES)
            out_ref[sl, :] = (p[:, sl] * scale).T.astype(BF16)
    ga_ref[...] = jax.nn.sigmoid(proj(COL_GATES, D_MODEL)).astype(BF16)
    gb_ref[...] = jax.nn.sigmoid(proj(COL_GATES + D_MODEL, D_MODEL)).astype(BF16)

    lane = lax.broadcasted_iota(I32, (tm, LANES), 1)
    ones_lanes = jnp.where(lane < N_PIECES, 1.0, 0.0)
    small = proj(COL_SMALL, 2 * LANES)
    kk_ref[...] = jnp.concatenate([rope(small[:, :LANES]), ones_lanes], axis=1).astype(BF16)
    blk = small[:, LANES:]

    is_f = (lane >= LANE_FORGET) & (lane < LANE_FORGET + N_HEADS)
    z = blk + fb_ref[...]
    logf = jnp.minimum(z, 0.0) - jnp.log1p(jnp.exp(-jnp.abs(z)))
    logf = jnp.where(is_f, logf, 0.0)
    tri = (lax.broadcasted_iota(I32, (tm, tm), 0)
           >= lax.broadcasted_iota(I32, (tm, tm), 1)).astype(BF16)
    p0, p1, p2 = _split3(logf)
    cum = (jnp.dot(tri, p0, preferred_element_type=F32)
           + jnp.dot(tri, p1, preferred_element_type=F32)
           + jnp.dot(tri, p2, preferred_element_type=F32)) + carry_scr[0:1, :]
    carry_scr[0:1, :] = cum[tm - 1:tm, :]
    cum = cum * LOG2E

    c0, c1, c2 = _split3(cum)
    kaux = (jnp.dot(c0, place_ref[0], preferred_element_type=F32)
            + jnp.dot(c1, place_ref[1], preferred_element_type=F32)
            + jnp.dot(c2, place_ref[2], preferred_element_type=F32) + ones_lanes).astype(BF16)
    kf = proj(COL_KF, W_HEADS).astype(BF16)
    for pair in range(N_HEADS // 2):
        kf_ref[:, 2 * pair * LANES:(2 * pair + 1) * LANES] = kf[:, pair * LANES:(pair + 1) * LANES]
        kf_ref[:, (2 * pair + 1) * LANES:(2 * pair + 2) * LANES] = kaux

    is_w = (lane >= LANE_WIDX) & (lane < LANE_WIDX + N_HEADS)
    comb = jnp.where(is_w, blk * (N_HEADS ** -0.5), jnp.where(is_f, cum, blk))
    comb_t = jnp.where(lane == LANE_ONES, 1.0, comb).T
    vat_ref[...] = comb_t.astype(BF16)
    auxt_ref[...] = comb_t[LANE_WIDX:LANE_WIDX + 2 * N_HEADS]


def _in_call(x, sh, sc, gain, pos, invf, fb, place, w_all, *, tm):
    B, S, _ = x.shape
    hm_t = jax.ShapeDtypeStruct((B, N_HEADS, HEAD_DIM, S), BF16)
    wide_t = jax.ShapeDtypeStruct((B, W_HEADS, S), BF16)
    gate = jax.ShapeDtypeStruct((B, S, D_MODEL), BF16)
    row = lambda b, s: (b, s, 0)
    col = lambda b, s: (b, 0, s)
    vec = pl.BlockSpec((None, 1, D_MODEL), lambda b, s: (b, 0, 0))
    const2 = lambda b, s: (0, 0)
    hm_spec = pl.BlockSpec((None, N_HEADS, HEAD_DIM, tm), lambda b, s: (b, 0, 0, s))
    return pl.pallas_call(
        functools.partial(_in_kernel, tm=tm),
        grid=(B, S // tm),
        in_specs=[
            pl.BlockSpec((None, tm, D_MODEL), row),
            vec, vec,
            pl.BlockSpec((1, D_MODEL), const2),
            pl.BlockSpec((None, tm, 1), row),
            pl.BlockSpec((1, LANES), const2),
            pl.BlockSpec((1, LANES), const2),
            pl.BlockSpec((N_PIECES, LANES, LANES), lambda b, s: (0, 0, 0)),
            pl.BlockSpec((D_MODEL, N_COLS), const2, pipeline_mode=pl.Buffered(1)),
        ],
        out_specs=[
            hm_spec, hm_spec,
            pl.BlockSpec((None, W_HEADS, tm), col),
            pl.BlockSpec((None, tm, 2 * W_HEADS), row),
            pl.BlockSpec((None, W_HEADS, tm), col),
            pl.BlockSpec((None, tm, D_MODEL), row),
            pl.BlockSpec((None, tm, D_MODEL), row),
            pl.BlockSpec((None, tm, 2 * LANES), row),
            pl.BlockSpec((None, LANES, tm), col),
            pl.BlockSpec((None, 2 * N_HEADS, tm), col),
        ],
        out_shape=[hm_t, hm_t, wide_t,
                   jax.ShapeDtypeStruct((B, S, 2 * W_HEADS), BF16),
                   wide_t, gate, gate,
                   jax.ShapeDtypeStruct((B, S, 2 * LANES), BF16),
                   jax.ShapeDtypeStruct((B, LANES, S), BF16),
                   jax.ShapeDtypeStruct((B, 2 * N_HEADS, S), F32)],
        scratch_shapes=[pltpu.VMEM((SUBLANES, LANES), F32)],
        compiler_params=_params(2),
        name="in_proj",
    )(x, sh, sc, gain, pos, invf, fb, place, w_all)


def _dsa_kernel(qit_ref, qat_ref, aux_ref, kk_ref, vat_ref, o_ref,
                keys_scr, acc_scr, p_scr, rhs_scr, *, q_blk, tk, topk, idx_bits):
    i = pl.program_id(1)
    width = N_HEADS * q_blk
    n_tiles = ((i + 1) * q_blk + tk - 1) // tk

    zeros = jnp.zeros((HEAD_DIM, width), BF16)
    qi_ext = jnp.concatenate(
        [jnp.concatenate([qit_ref[h] for h in range(N_HEADS)], axis=1), zeros], axis=0)
    rhs_scr[...] = jnp.concatenate(
        [zeros, jnp.concatenate([qat_ref[h] for h in range(N_HEADS)], axis=1),
         jnp.zeros((LANES, width), BF16)], axis=0)
    piece_row = lax.broadcasted_iota(I32, (PIECE_ROWS, width), 0)

    def set_row_term(r):
        hi = r.astype(BF16).astype(F32)
        mid = (r - hi).astype(BF16).astype(F32)
        lo = r - hi - mid
        blk = jnp.where(piece_row == 0, hi, jnp.where(piece_row == 1, mid,
                                                      jnp.where(piece_row == 2, lo, 0.0)))
        rhs_scr[2 * HEAD_DIM:2 * HEAD_DIM + PIECE_ROWS, :] = blk.astype(BF16)
    w_rows = aux_ref[0:N_HEADS, :]
    q_pos = i * q_blk + lax.broadcasted_iota(I32, (1, q_blk), 1)
    key_lim = (q_pos // CHUNK + 1) * CHUNK
    row_iota = lax.broadcasted_iota(I32, (tk, q_blk), 0)

    def score_tile(t, _):
        off = pl.multiple_of(t * tk, tk)
        s = jnp.dot(kk_ref[pl.ds(off, tk), 0:LANES], qi_ext,
                    preferred_element_type=F32)
        acc = jnp.zeros((tk, q_blk), F32)
        for h in range(N_HEADS):
            acc = acc + w_rows[h:h + 1, :] * jnp.maximum(s[:, h * q_blk:(h + 1) * q_blk], 0.0)
        bits = pltpu.bitcast(acc, I32)
        key = bits ^ ((bits >> 31) & 0x7FFFFFFF)
        keys_scr[pl.ds(off, tk), :] = jnp.where(off + row_iota < key_lim, key, INT_MIN)
        return 0

    lax.fori_loop(0, n_tiles, score_tile, 0)

    def count(pred):
        def tile(t, acc):
            off = pl.multiple_of(t * tk, tk)
            x = pred(keys_scr[pl.ds(off, tk), :], off + row_iota)
            return acc + jnp.sum(x.reshape(tk // 64, 64, q_blk), axis=0)
        acc = lax.fori_loop(0, n_tiles, tile, jnp.zeros((64, q_blk), I32))
        return _col_reduce(acc, jnp.sum)

    def radix_pass(b, carry):
        prefix, cnt_ge = carry
        cand = prefix | jnp.left_shift(jnp.int32(1), 31 - b)
        cand_s = cand ^ INT_MIN
        cnt = count(lambda k, _: jnp.where(k >= cand_s, 1, 0))
        ok = cnt >= topk
        return jnp.where(ok, cand, prefix), jnp.where(ok, cnt, cnt_ge)

    total = jnp.zeros((1, q_blk), I32) + n_tiles * tk
    prefix, cnt_ge = lax.fori_loop(0, 32, radix_pass, (jnp.zeros((1, q_blk), I32), total))
    kth = prefix ^ INT_MIN
    thr = jnp.maximum(kth, MIN_VALID_KEY)

    has_tie = jnp.where(cnt_ge > topk, jnp.where(kth >= MIN_VALID_KEY, 1, 0), 0)

    @pl.when(jnp.max(has_tie) > 0)
    def _():
        n_gt = count(lambda k, _: jnp.where(k > thr, 1, 0))
        need = topk - n_gt

        def idx_pass(b, pre):
            cand = pre | jnp.left_shift(jnp.int32(1), idx_bits - 1 - b)
            c = count(lambda k, kidx: jnp.where(k == thr, jnp.where(kidx < cand, 1, 0), 0))
            return jnp.where(c < need, cand, pre)

        last = lax.fori_loop(0, idx_bits, idx_pass, jnp.zeros((1, q_blk), I32))

        def drop_tile(t, _):
            off = pl.multiple_of(t * tk, tk)
            k = keys_scr[pl.ds(off, tk), :]
            drop = jnp.where(k == thr, jnp.where(off + row_iota > last, INT_MIN, k), k)
            keys_scr[pl.ds(off, tk), :] = drop
            return 0

        lax.fori_loop(0, n_tiles, drop_tile, 0)

    acc_scr[...] = jnp.zeros_like(acc_scr)

    heads = [slice(h * q_blk, (h + 1) * q_blk) for h in range(N_HEADS)]

    def lane_row(x8):
        return jnp.concatenate([x8[h:h + 1, :] for h in range(N_HEADS)], axis=1)

    def tile_operands(t):
        off = pl.multiple_of(t * tk, tk)
        bias = jnp.where(keys_scr[pl.ds(off, tk), :] >= thr, 0.0, NEG_INF)
        return kk_ref[pl.ds(off, tk), :], bias, vat_ref[:, pl.ds(off, tk)]

    def attn_tile(t, m_all):
        kt, bias, vt = tile_operands(t)
        set_row_term(jnp.zeros((1, width), F32))
        logits = jnp.dot(kt, rhs_scr[...], preferred_element_type=F32)
        m_rows = []
        for h in range(N_HEADS):
            s = logits[:, heads[h]] + bias
            m_new = jnp.maximum(m_all[h:h + 1, :], _col_reduce(s, jnp.max))
            p_scr[:, heads[h]] = jnp.exp2(s - m_new).astype(BF16)
            m_rows.append(m_new)
        m_new = jnp.concatenate(m_rows, axis=0)
        pv = jnp.dot(vt, p_scr[...], preferred_element_type=F32)
        acc_scr[...] = acc_scr[...] * lane_row(jnp.exp2(m_all - m_new)) + pv
        return m_new

    def one_pass_tile(t, carry):
        m_ref, worst = carry
        kt, bias, vt = tile_operands(t)
        set_row_term(-lane_row(m_ref))
        x_all = jnp.dot(kt, rhs_scr[...], preferred_element_type=F32)
        tmax = []
        for h in range(N_HEADS):
            x = x_all[:, heads[h]] + bias
            p_scr[:, heads[h]] = jnp.exp2(x).astype(BF16)
            tmax.append(_col_reduce(x, jnp.max))
        tmax = jnp.concatenate(tmax, axis=0)
        up = jnp.maximum(tmax, 0.0)
        pv = jnp.dot(vt, p_scr[...], preferred_element_type=F32)
        acc_scr[...] = (acc_scr[...] + pv) * lane_row(jnp.exp2(-up))
        return m_ref + up, jnp.maximum(worst, tmax)

    m0 = jnp.full((N_HEADS, q_blk), M_INIT, F32)
    m1 = attn_tile(0, m0)
    _, worst = lax.fori_loop(1, n_tiles, one_pass_tile, (m1, m0))

    @pl.when(jnp.max(worst) > MAX_EXP_ARG)
    def _():
        acc_scr[...] = jnp.zeros_like(acc_scr)
        lax.fori_loop(0, n_tiles, attn_tile, m0)

    denom = acc_scr[LANE_ONES:LANE_ONES + 1, :]
    for pair in range(N_HEADS // 2):
        halves = []
        for h in (2 * pair, 2 * pair + 1):
            halves.append(acc_scr[0:HEAD_DIM, heads[h]] / denom[:, heads[h]])
        o_ref[:, pair * LANES:(pair + 1) * LANES] = (
            jnp.concatenate(halves, axis=0).T.astype(BF16))


def _dsa_call(qit, qat, auxt, kk, vat, *, q_blk, tk):
    B, _, _, S = qit.shape
    topk = min(TOPK_MAX, S // 4)
    idx_bits = max(1, int(np.ceil(np.log2(S))))
    width = N_HEADS * q_blk
    hm_spec = pl.BlockSpec((None, N_HEADS, HEAD_DIM, q_blk), lambda b, i: (b, 0, 0, i))
    return pl.pallas_call(
        functools.partial(_dsa_kernel, q_blk=q_blk, tk=tk, topk=topk, idx_bits=idx_bits),
        grid=(B, S // q_blk),
        in_specs=[
            hm_spec, hm_spec,
            pl.BlockSpec((None, 2 * N_HEADS, q_blk), lambda b, i: (b, 0, i)),
            pl.BlockSpec((None, S, 2 * LANES), lambda b, i: (b, 0, 0)),
            pl.BlockSpec((None, LANES, S), lambda b, i: (b, 0, 0)),
        ],
        out_specs=pl.BlockSpec((None, q_blk, W_HEADS), lambda b, i: (b, i, 0)),
        out_shape=jax.ShapeDtypeStruct((B, S, W_HEADS), BF16),
        scratch_shapes=[
            pltpu.VMEM((S, q_blk), I32),
            pltpu.VMEM((LANES, width), F32),
            pltpu.VMEM((tk, width), BF16),
            pltpu.VMEM((2 * LANES, width), BF16),
        ],
        compiler_params=_params(2),
        name="dsa",
    )(qit, qat, auxt, kk, vat)


def _fox_kernel(qi_tbl, kj_tbl, qt_ref, k_ref, vt_ref, auxq_ref, auxk_ref, neg_ref, o_ref,
                m_scr, acc_scr, pv_scr, done_scr, *, t):
    step = pl.program_id(1)
    i = qi_tbl[step]
    j = kj_tbl[step]
    top = lax.broadcasted_iota(I32, (LANES, t), 0) < HEAD_DIM
    piece_row = lax.broadcasted_iota(I32, (PIECE_ROWS, t), 0)
    ones_rows = jnp.ones((PIECE_ROWS, t), BF16)

    @pl.when(j == 0)
    def _():
        m_scr[...] = jnp.minimum(auxq_ref[N_HEADS:2 * N_HEADS, :]
                                 - auxk_ref[N_HEADS:2 * N_HEADS, t - 1:t], 0.0)
        acc_scr[...] = jnp.zeros_like(acc_scr)

    def head_operands(h, r):
        pair, e = divmod(h, 2)
        qp = qt_ref[pair * LANES:(pair + 1) * LANES, :]
        qm = jnp.where(top, qp, 0) if e == 0 else jnp.where(top, 0, qp)
        hi = r.astype(BF16).astype(F32)
        mid = (r - hi).astype(BF16).astype(F32)
        lo = r - hi - mid
        pieces = jnp.where(piece_row == 0, hi, jnp.where(piece_row == 1, mid,
                                                         jnp.where(piece_row == 2, lo, 0.0)))
        rhs = jnp.concatenate([qm, pieces.astype(BF16), neg_ref[h]], axis=0)
        v_ext = jnp.concatenate([vt_ref[h * HEAD_DIM:(h + 1) * HEAD_DIM, :], ones_rows], axis=0)
        return k_ref[:, 2 * pair * LANES:(2 * pair + 2) * LANES], rhs, v_ext

    def two_pass_body():
        causal = (j * t + lax.broadcasted_iota(I32, (t, t), 0)
                  <= i * t + lax.broadcasted_iota(I32, (t, t), 1))
        for h in range(N_HEADS):
            lhs, rhs, v_ext = head_operands(h, auxq_ref[N_HEADS + h:N_HEADS + h + 1, :])
            s = jnp.dot(lhs, rhs, preferred_element_type=F32)
            s = jnp.where(causal, s, NEG_INF)
            m_prev = jnp.where(j == 0, M_INIT, m_scr[h:h + 1, :])
            m_new = jnp.maximum(m_prev, _col_reduce(s, jnp.max))
            p = jnp.exp2(s - m_new).astype(BF16)
            m_scr[h:h + 1, :] = m_new
            acc_scr[h] = (acc_scr[h] * jnp.exp2(m_prev - m_new)
                          + jnp.dot(v_ext, p, preferred_element_type=F32))

    def one_pass_body(diag):
        if diag:
            causal = (lax.broadcasted_iota(I32, (t, t), 0)
                      <= lax.broadcasted_iota(I32, (t, t), 1))
        tmax, shifts = [], []
        for pair in range(N_HEADS // 2):
            ops = []
            for h in (2 * pair, 2 * pair + 1):
                cq = auxq_ref[N_HEADS + h:N_HEADS + h + 1, :]
                ck_first = auxk_ref[N_HEADS + h:N_HEADS + h + 1, 0:1]
                ck_last = auxk_ref[N_HEADS + h:N_HEADS + h + 1, t - 1:t]
                shift = jnp.minimum(cq - ck_last, 0.0) - jnp.minimum(cq - ck_first, 0.0)
                shift = jnp.where(j == 0, 0.0, shift)
                shifts.append(shift)
                ops.append(head_operands(h, cq - (m_scr[h:h + 1, :] + shift)))
            x2 = jnp.dot(ops[0][0], jnp.concatenate([ops[0][1], ops[1][1]], axis=1),
                         preferred_element_type=F32)
            for e in range(2):
                x = x2[:, e * t:(e + 1) * t]
                if diag:
                    x = jnp.where(causal, x, NEG_INF)
                tmax.append(_col_reduce(x, jnp.max))
                pv_scr[2 * pair + e] = jnp.dot(ops[e][2], jnp.exp2(x).astype(BF16),
                                               preferred_element_type=F32)
        tmax_all = jnp.concatenate(tmax, axis=0)
        ok = jnp.logical_and(jnp.max(tmax_all) <= MAX_EXP_ARG, jnp.min(tmax_all) >= -MAX_EXP_ARG)

        @pl.when(ok)
        def _():
            for h in range(N_HEADS):
                up = jnp.maximum(tmax[h], 0.0)
                m_scr[h:h + 1, :] = m_scr[h:h + 1, :] + shifts[h] + up
                acc_scr[h] = (acc_scr[h] * jnp.exp2(-shifts[h]) + pv_scr[h]) * jnp.exp2(-up)

        return jnp.where(ok, 1, 0)

    done_scr[0] = 0

    @pl.when(j < i)
    def _():
        done_scr[0] = one_pass_body(False)

    @pl.when(j == i)
    def _():
        done_scr[0] = one_pass_body(True)

    @pl.when(done_scr[0] == 0)
    def _():
        two_pass_body()

    @pl.when(j == i)
    def _():
        for pair in range(N_HEADS // 2):
            halves = [acc_scr[h, 0:HEAD_DIM, :] / acc_scr[h, HEAD_DIM:HEAD_DIM + 1, :]
                      for h in (2 * pair, 2 * pair + 1)]
            o_ref[:, pair * LANES:(pair + 1) * LANES] = (
                jnp.concatenate(halves, axis=0).T.astype(BF16))


def _fox_call(qt, k, vt, auxt, *, t):
    B, S, _ = k.shape
    n = S // t
    neg = np.zeros((N_HEADS, LANES - PIECE_ROWS, t), np.float32)
    for h in range(N_HEADS):
        neg[h, N_PIECES * h:N_PIECES * (h + 1), :] = -1.0
    qi_tbl = np.concatenate([np.full(i + 1, i, np.int32) for i in range(n)])
    kj_tbl = np.concatenate([np.arange(i + 1, dtype=np.int32) for i in range(n)])
    grid_spec = pltpu.PrefetchScalarGridSpec(
        num_scalar_prefetch=2,
        grid=(B, len(qi_tbl)),
        in_specs=[
            pl.BlockSpec((None, W_HEADS, t), lambda b, s, qi, kj: (b, 0, qi[s])),
            pl.BlockSpec((None, t, 2 * W_HEADS), lambda b, s, qi, kj: (b, kj[s], 0)),
            pl.BlockSpec((None, W_HEADS, t), lambda b, s, qi, kj: (b, 0, kj[s])),
            pl.BlockSpec((None, 2 * N_HEADS, t), lambda b, s, qi, kj: (b, 0, qi[s])),
            pl.BlockSpec((None, 2 * N_HEADS, t), lambda b, s, qi, kj: (b, 0, kj[s])),
            pl.BlockSpec((N_HEADS, LANES - PIECE_ROWS, t), lambda b, s, qi, kj: (0, 0, 0)),
        ],
        out_specs=pl.BlockSpec((None, t, W_HEADS), lambda b, s, qi, kj: (b, qi[s], 0)),
        scratch_shapes=[
            pltpu.VMEM((N_HEADS, t), F32),
            pltpu.VMEM((N_HEADS, HEAD_DIM + PIECE_ROWS, t), F32),
            pltpu.VMEM((N_HEADS, HEAD_DIM + PIECE_ROWS, t), F32),
            pltpu.SMEM((1,), I32),
        ],
    )
    return pl.pallas_call(
        functools.partial(_fox_kernel, t=t),
        grid_spec=grid_spec,
        out_shape=jax.ShapeDtypeStruct((B, S, W_HEADS), BF16),
        compiler_params=_params(2),
        name="fox",
    )(jnp.asarray(qi_tbl), jnp.asarray(kj_tbl), qt, k, vt, auxt, auxt, jnp.asarray(neg, BF16))


def _merge_kernel(oa_ref, ob_ref, ga_ref, gb_ref, x_ref, g1_ref, sc_ref, sh_ref, n2_ref,
                  wa_ref, wb_ref, wo_ref, x1_ref, h2_ref):
    ya = jnp.dot(oa_ref[...], wa_ref[...], preferred_element_type=F32)
    yb = jnp.dot(ob_ref[...], wb_ref[...], preferred_element_type=F32)
    mix = ga_ref[...].astype(F32) * ya + gb_ref[...].astype(F32) * yb
    y = jnp.dot(mix.astype(BF16), wo_ref[...], preferred_element_type=F32)
    x1 = x_ref[...] + g1_ref[...] * y
    x1_ref[...] = x1
    h2_ref[...] = _rms_mod(x1, n2_ref[...], sc_ref[...], sh_ref[...]).astype(BF16)


def _merge_call(oa, ob, ga, gb, x, g1, sc2, sh2, n2, wa, wb, wo, *, tm):
    B, S, _ = x.shape
    row = lambda b, s: (b, s, 0)
    const2 = lambda b, s: (0, 0)
    vec = pl.BlockSpec((None, 1, D_MODEL), lambda b, s: (b, 0, 0))
    return pl.pallas_call(
        _merge_kernel,
        grid=(B, S // tm),
        in_specs=[
            pl.BlockSpec((None, tm, W_HEADS), row),
            pl.BlockSpec((None, tm, W_HEADS), row),
            pl.BlockSpec((None, tm, D_MODEL), row),
            pl.BlockSpec((None, tm, D_MODEL), row),
            pl.BlockSpec((None, tm, D_MODEL), row),
            vec, vec, vec,
            pl.BlockSpec((1, D_MODEL), const2),
            pl.BlockSpec((W_HEADS, D_MODEL), const2),
            pl.BlockSpec((W_HEADS, D_MODEL), const2),
            pl.BlockSpec((D_MODEL, D_MODEL), const2),
        ],
        out_specs=[pl.BlockSpec((None, tm, D_MODEL), row),
                   pl.BlockSpec((None, tm, D_MODEL), row)],
        out_shape=[jax.ShapeDtypeStruct((B, S, D_MODEL), F32),
                   jax.ShapeDtypeStruct((B, S, D_MODEL), BF16)],
        compiler_params=_params(2),
        name="merge",
    )(oa, ob, ga, gb, x, g1, sc2, sh2, n2, wa, wb, wo)


def _ffn_kernel(h_ref, x_ref, g2_ref, wup_ref, cw_ref, cb_ref, wdn_ref, fg_ref, o_ref,
                carry_scr, *, tm, fc, final):
    s_idx = pl.program_id(1)

    @pl.when(s_idx == 0)
    def _():
        carry_scr[...] = jnp.zeros_like(carry_scr)

    hb = h_ref[...]
    row = lax.broadcasted_iota(I32, (SUBLANES, fc), 0)
    acc = jnp.zeros((tm, D_MODEL), F32)
    for cblk in range(D_FF // fc):
        halves = []
        for part in range(2):
            col = part * D_FF + cblk * fc
            u = jnp.dot(hb, wup_ref[:, col:col + fc], preferred_element_type=F32)
            prev = carry_scr[:, col:col + fc]
            r1 = pltpu.roll(u, 1, 0)
            r2 = pltpu.roll(u, 2, 0)
            head1 = jnp.where(row == 0, prev[7:8], r1[:SUBLANES])
            head2 = jnp.where(row == 0, prev[6:7], jnp.where(row == 1, prev[7:8], r2[:SUBLANES]))
            u1 = jnp.concatenate([head1, r1[SUBLANES:]], axis=0)
            u2 = jnp.concatenate([head2, r2[SUBLANES:]], axis=0)
            carry_scr[:, col:col + fc] = u[tm - SUBLANES:tm]
            cw = cw_ref[:, col:col + fc]
            halves.append(cb_ref[:, col:col + fc] + (cw[0:1] * u2 + cw[1:2] * u1 + cw[2:3] * u))
        act = (jax.nn.silu(halves[0]) * halves[1]).astype(BF16)
        acc = acc + jnp.dot(act, wdn_ref[cblk * fc:(cblk + 1) * fc, :],
                            preferred_element_type=F32)
    x2 = x_ref[...] + g2_ref[...] * acc
    if final:
        var = jnp.mean(x2 * x2, axis=-1, keepdims=True)
        x2 = x2 * lax.rsqrt(var + EPS) * fg_ref[...]
    o_ref[...] = x2


def _ffn_call(h2, x1, g2, wup, cw, cb, wdn, fg, *, tm, fc, final):
    B, S, _ = x1.shape
    row = lambda b, s: (b, s, 0)
    const2 = lambda b, s: (0, 0)
    return pl.pallas_call(
        functools.partial(_ffn_kernel, tm=tm, fc=fc, final=final),
        grid=(B, S // tm),
        in_specs=[
            pl.BlockSpec((None, tm, D_MODEL), row),
            pl.BlockSpec((None, tm, D_MODEL), row),
            pl.BlockSpec((None, 1, D_MODEL), lambda b, s: (b, 0, 0)),
            pl.BlockSpec((D_MODEL, 2 * D_FF), const2, pipeline_mode=pl.Buffered(1)),
            pl.BlockSpec((SUBLANES, 2 * D_FF), const2),
            pl.BlockSpec((1, 2 * D_FF), const2),
            pl.BlockSpec((D_FF, D_MODEL), const2, pipeline_mode=pl.Buffered(1)),
            pl.BlockSpec((1, D_MODEL), const2),
        ],
        out_specs=pl.BlockSpec((None, tm, D_MODEL), row),
        out_shape=jax.ShapeDtypeStruct((B, S, D_MODEL), F32),
        scratch_shapes=[pltpu.VMEM((SUBLANES, 2 * D_FF), F32)],
        compiler_params=_params(2),
        name="ffn",
    )(h2, x1, g2, wup, cw, cb, wdn, fg)


def _reorder_w_in(w):
    o = np.cumsum([0, W_HEADS, HEAD_DIM, HEAD_DIM, W_HEADS, HEAD_DIM, N_HEADS,
                   W_HEADS, W_HEADS, W_HEADS, N_HEADS, 2 * D_MODEL])
    seg = lambda k: w[:, o[k]:o[k + 1]]
    q_a, k_a, v_a, q_i, k_i, w_i, q_f, k_f, v_f, f_f, gates = [seg(k) for k in range(11)]
    pad = jnp.zeros((w.shape[0], N_COLS - int(o[-1])), w.dtype)
    return jnp.concatenate([q_a, q_i, q_f, k_f, v_f, gates, k_i, k_a, v_a, w_i, f_f, pad],
                           axis=1).astype(BF16)


def kernel(x, c, positions, mod_w, mod_b, norm1_g, norm2_g, w_in, forget_bias, w_branch_a,
           w_branch_b, w_out, w_up, conv_w, conv_b, w_down, final_g):
    B, S, _ = x.shape
    depth = mod_w.shape[0]
    tm = min(TOKEN_TILE, S)
    mod = _mod_call(c, mod_w, mod_b)[:, :, :B].reshape(depth, 6, B, 1, D_MODEL)
    pos = positions.astype(F32).reshape(B, S, 1)

    inv_freq = ROPE_THETA ** (-jnp.arange(0, ROT_DIM, 2, dtype=F32) / ROT_DIM)
    l64 = np.arange(LANES) % HEAD_DIM
    invf = jnp.where(l64 < ROT_DIM, inv_freq[l64 % (ROT_DIM // 2)], 0.0).reshape(1, LANES)

    place = np.zeros((N_PIECES, LANES, LANES), np.float32)
    for h in range(N_HEADS):
        for piece in range(N_PIECES):
            place[piece, LANE_FORGET + h, LANE_CK + N_PIECES * h + piece] = 1.0
    place = jnp.asarray(place, BF16)

    for l in range(depth):
        sh1, sc1, g1, sh2, sc2, g2 = [mod[l, k] for k in range(6)]
        fb = jnp.zeros((1, LANES), F32).at[0, LANE_FORGET:LANE_FORGET + N_HEADS].set(forget_bias[l])
        (qat, qit, qft, kf, vft, ga, gb, kk, vat, auxt) = _in_call(
            x, sh1, sc1, norm1_g[l].reshape(1, D_MODEL), pos, invf, fb, place,
            _reorder_w_in(w_in[l]), tm=tm)
        oa = _dsa_call(qit, qat, auxt, kk, vat, q_blk=DSA_Q_BLOCK, tk=min(DSA_KEY_TILE, S))
        ob = _fox_call(qft, kf, vft, auxt, t=tm)
        x1, h2 = _merge_call(oa, ob, ga, gb, x, g1, sc2, sh2, norm2_g[l].reshape(1, D_MODEL),
                             w_branch_a[l].astype(BF16), w_branch_b[l].astype(BF16),
                             w_out[l].astype(BF16), tm=tm)
        cw8 = jnp.zeros((SUBLANES, 2 * D_FF), F32).at[:conv_w.shape[1]].set(conv_w[l])
        x = _ffn_call(h2, x1, g2, w_up[l].astype(BF16), cw8, conv_b[l].reshape(1, 2 * D_FF),
                      w_down[l].astype(BF16), final_g.reshape(1, D_MODEL),
                      tm=tm, fc=FFN_CHUNK, final=(l == depth - 1))
    return x
```

```python
import functools

import numpy as np
import jax
import jax.numpy as jnp
from jax import lax
from jax.experimental import pallas as pl
from jax.experimental.pallas import tpu as pltpu

F32 = jnp.float32
BF16 = jnp.bfloat16
I32 = jnp.int32

D_MODEL = 1024
CHUNK = 64
HEAD_DIM = 64
N_HEADS = 8
TOPK_MAX = 256
ROPE_THETA = 500000.0
ROT_DIM = HEAD_DIM // 4
D_FF = 2816
EPS = 1e-6
NEG_INF = -1e30
W_HEADS = N_HEADS * HEAD_DIM

LANES = 128
SUBLANES = 8
VMEM_LIMIT = 56 * 1024 * 1024

TOKEN_TILE = 512
DSA_Q_BLOCK = 2 * LANES
DSA_KEY_TILE = 512
COUNT_ROWS = 64
FFN_CHUNK = 2816

COL_QA = 0
COL_QI = 512
COL_QF = 1024
COL_KF = 1536
COL_VF = 2048
COL_GATES = 2560
COL_SMALL = 4608
N_COLS = 4864
LANE_WIDX = 64
LANE_FORGET = 72
LANE_ONES = 80
N_PIECES = 3
PIECE_ROWS = 16
LANE_CK = PIECE_ROWS

INT_MIN = -(2 ** 31)
M_INIT = -3.0e38
LOG2E = 1.4426950408889634
MAX_EXP_ARG = 100.0


def _f32_key(v):
    b = int(np.float32(v).view(np.int32))
    return b ^ ((b >> 31) & 0x7FFFFFFF)


MIN_VALID_KEY = _f32_key(0.5 * NEG_INF) + 1


def _params(n_grid):
    return pltpu.CompilerParams(
        dimension_semantics=("arbitrary",) * n_grid, vmem_limit_bytes=VMEM_LIMIT)


def _col_reduce(x, op):
    rows = x.shape[0]
    part = op(x.reshape(rows // SUBLANES, SUBLANES, x.shape[1]), axis=0)
    return op(part, axis=0, keepdims=True)


def _mod_kernel(c_ref, w_ref, b_ref, o_ref):
    o_ref[...] = jnp.dot(c_ref[...], w_ref[...], precision=lax.Precision.HIGHEST,
                         preferred_element_type=F32) + b_ref[...]


def _mod_call(c, mod_w, mod_b):
    depth = mod_w.shape[0]
    c8 = jnp.zeros((SUBLANES, D_MODEL), F32).at[: c.shape[0]].set(c)
    b4 = mod_b.reshape(depth, 6, 1, D_MODEL)
    return pl.pallas_call(
        _mod_kernel,
        grid=(depth, 6),
        in_specs=[
            pl.BlockSpec((SUBLANES, D_MODEL), lambda l, j: (0, 0)),
            pl.BlockSpec((None, D_MODEL, D_MODEL), lambda l, j: (l, 0, j)),
            pl.BlockSpec((None, None, 1, D_MODEL), lambda l, j: (l, j, 0, 0)),
        ],
        out_specs=pl.BlockSpec((None, None, SUBLANES, D_MODEL), lambda l, j: (l, j, 0, 0)),
        out_shape=jax.ShapeDtypeStruct((depth, 6, SUBLANES, D_MODEL), F32),
        compiler_params=_params(2),
        name="mod",
    )(c8, mod_w, b4)


def _rms_mod(x, gain, scale, shift):
    var = jnp.mean(x * x, axis=-1, keepdims=True)
    y = x * lax.rsqrt(var + EPS) * gain
    return y * (1.0 + scale) + shift


def _split3(x):
    p0 = x.astype(BF16)
    r1 = x - p0.astype(F32)
    p1 = r1.astype(BF16)
    return p0, p1, (r1 - p1.astype(F32)).astype(BF16)


def _in_kernel(x_ref, sh_ref, sc_ref, g_ref, pos_ref, invf_ref, fb_ref, place_ref, w_ref,
               qat_ref, qit_ref, qft_ref, kf_ref, vft_ref, ga_ref, gb_ref,
               kk_ref, vat_ref, auxt_ref, carry_scr, *, tm):
    s_idx = pl.program_id(1)

    @pl.when(s_idx == 0)
    def _():
        carry_scr[...] = jnp.zeros_like(carry_scr)

    h = _rms_mod(x_ref[...], g_ref[...], sc_ref[...], sh_ref[...])
    hb = h.astype(BF16)

    ang = pos_ref[...] * invf_ref[...]
    cos = jnp.cos(ang)
    sin = jnp.sin(ang)
    l64 = lax.broadcasted_iota(I32, (tm, LANES), 1) & (HEAD_DIM - 1)
    half = ROT_DIM // 2
    t_cos = jnp.where(l64 < ROT_DIM, cos, 1.0)
    t_lo = jnp.where(l64 < half, -sin, 0.0)
    t_hi = jnp.where(l64 < half, 0.0, jnp.where(l64 < ROT_DIM, sin, 0.0))

    def rope(xc):
        return (xc * t_cos + pltpu.roll(xc, LANES - half, 1) * t_lo
                + pltpu.roll(xc, half, 1) * t_hi)

    def proj(col, width):
        return jnp.dot(hb, w_ref[:, col:col + width], preferred_element_type=F32)

    for col, out_ref, scale in ((COL_QA, qat_ref, 0.125 * LOG2E), (COL_QI, qit_ref, 0.125)):
        p = proj(col, W_HEADS)
        for cblk in range(W_HEADS // LANES):
            r = (rope(p[:, cblk * LANES:(cblk + 1) * LANES]) * scale).T.astype(BF16)
            out_ref[2 * cblk] = r[:HEAD_DIM]
            out_ref[2 * cblk + 1] = r[HEAD_DIM:]

    for col, out_ref, scale in ((COL_QF, qft_ref, 0.125 * LOG2E), (COL_VF, vft_ref, 1.0)):
        p = proj(col, W_HEADS)
        for cblk in range(W_HEADS // LANES):
            sl = slice(cblk * LANES, (cblk + 1) * LANES)
            out_ref[sl, :] = (p[:, sl] * scale).T.astype(BF16)
    ga_ref[...] = jax.nn.sigmoid(proj(COL_GATES, D_MODEL)).astype(BF16)
    gb_ref[...] = jax.nn.sigmoid(proj(COL_GATES + D_MODEL, D_MODEL)).astype(BF16)

    lane = lax.broadcasted_iota(I32, (tm, LANES), 1)
    ones_lanes = jnp.where(lane < N_PIECES, 1.0, 0.0)
    small = proj(COL_SMALL, 2 * LANES)
    kk_ref[...] = jnp.concatenate([rope(small[:, :LANES]), ones_lanes], axis=1).astype(BF16)
    blk = small[:, LANES:]

    is_f = (lane >= LANE_FORGET) & (lane < LANE_FORGET + N_HEADS)
    z = blk + fb_ref[...]
    logf = jnp.minimum(z, 0.0) - jnp.log1p(jnp.exp(-jnp.abs(z)))
    logf = jnp.where(is_f, logf, 0.0)
    tri = (lax.broadcasted_iota(I32, (tm, tm), 0)
           >= lax.broadcasted_iota(I32, (tm, tm), 1)).astype(BF16)
    p0, p1, p2 = _split3(logf)
    cum = (jnp.dot(tri, p0, preferred_element_type=F32)
           + jnp.dot(tri, p1, preferred_element_type=F32)
           + jnp.dot(tri, p2, preferred_element_type=F32)) + carry_scr[0:1, :]
    carry_scr[0:1, :] = cum[tm - 1:tm, :]
    cum = cum * LOG2E

    c0, c1, c2 = _split3(cum)
    kaux = (jnp.dot(c0, place_ref[0], preferred_element_type=F32)
            + jnp.dot(c1, place_ref[1], preferred_element_type=F32)
            + jnp.dot(c2, place_ref[2], preferred_element_type=F32) + ones_lanes).astype(BF16)
    kf = proj(COL_KF, W_HEADS).astype(BF16)
    for pair in range(N_HEADS // 2):
        kf_ref[:, 2 * pair * LANES:(2 * pair + 1) * LANES] = kf[:, pair * LANES:(pair + 1) * LANES]
        kf_ref[:, (2 * pair + 1) * LANES:(2 * pair + 2) * LANES] = kaux

    is_w = (lane >= LANE_WIDX) & (lane < LANE_WIDX + N_HEADS)
    comb = jnp.where(is_w, blk * (N_HEADS ** -0.5), jnp.where(is_f, cum, blk))
    comb_t = jnp.where(lane == LANE_ONES, 1.0, comb).T
    vat_ref[...] = comb_t.astype(BF16)
    auxt_ref[...] = comb_t[LANE_WIDX:LANE_WIDX + 2 * N_HEADS]


def _in_call(x, sh, sc, gain, pos, invf, fb, place, w_all, *, tm):
    B, S, _ = x.shape
    hm_t = jax.ShapeDtypeStruct((B, N_HEADS, HEAD_DIM, S), BF16)
    wide_t = jax.ShapeDtypeStruct((B, W_HEADS, S), BF16)
    gate = jax.ShapeDtypeStruct((B, S, D_MODEL), BF16)
    row = lambda b, s: (b, s, 0)
    col = lambda b, s: (b, 0, s)
    vec = pl.BlockSpec((None, 1, D_MODEL), lambda b, s: (b, 0, 0))
    const2 = lambda b, s: (0, 0)
    hm_spec = pl.BlockSpec((None, N_HEADS, HEAD_DIM, tm), lambda b, s: (b, 0, 0, s))
    return pl.pallas_call(
        functools.partial(_in_kernel, tm=tm),
        grid=(B, S // tm),
        in_specs=[
            pl.BlockSpec((None, tm, D_MODEL), row),
            vec, vec,
            pl.BlockSpec((1, D_MODEL), const2),
            pl.BlockSpec((None, tm, 1), row),
            pl.BlockSpec((1, LANES), const2),
            pl.BlockSpec((1, LANES), const2),
            pl.BlockSpec((N_PIECES, LANES, LANES), lambda b, s: (0, 0, 0)),
            pl.BlockSpec((D_MODEL, N_COLS), const2, pipeline_mode=pl.Buffered(1)),
        ],
        out_specs=[
            hm_spec, hm_spec,
            pl.BlockSpec((None, W_HEADS, tm), col),
            pl.BlockSpec((None, tm, 2 * W_HEADS), row),
            pl.BlockSpec((None, W_HEADS, tm), col),
            pl.BlockSpec((None, tm, D_MODEL), row),
            pl.BlockSpec((None, tm, D_MODEL), row),
            pl.BlockSpec((None, tm, 2 * LANES), row),
            pl.BlockSpec((None, LANES, tm), col),
            pl.BlockSpec((None, 2 * N_HEADS, tm), col),
        ],
        out_shape=[hm_t, hm_t, wide_t,
                   jax.ShapeDtypeStruct((B, S, 2 * W_HEADS), BF16),
                   wide_t, gate, gate,
                   jax.ShapeDtypeStruct((B, S, 2 * LANES), BF16),
                   jax.ShapeDtypeStruct((B, LANES, S), BF16),
                   jax.ShapeDtypeStruct((B, 2 * N_HEADS, S), F32)],
        scratch_shapes=[pltpu.VMEM((SUBLANES, LANES), F32)],
        compiler_params=_params(2),
        name="in_proj",
    )(x, sh, sc, gain, pos, invf, fb, place, w_all)


def _dsa_kernel(qit_ref, qat_ref, aux_ref, kk_ref, vat_ref, o_ref,
                keys_scr, acc_scr, p_scr, rhs_scr, *, q_blk, tk, topk, idx_bits):
    i = pl.program_id(1)
    width = N_HEADS * q_blk
    n_tiles = ((i + 1) * q_blk + tk - 1) // tk

    zeros = jnp.zeros((HEAD_DIM, width), BF16)
    qi_ext = jnp.concatenate(
        [jnp.concatenate([qit_ref[h] for h in range(N_HEADS)], axis=1), zeros], axis=0)
    rhs_scr[...] = jnp.concatenate(
        [zeros, jnp.concatenate([qat_ref[h] for h in range(N_HEADS)], axis=1),
         jnp.zeros((LANES, width), BF16)], axis=0)
    piece_row = lax.broadcasted_iota(I32, (PIECE_ROWS, width), 0)

    def set_row_term(r):
        hi = r.astype(BF16).astype(F32)
        mid = (r - hi).astype(BF16).astype(F32)
        lo = r - hi - mid
        blk = jnp.where(piece_row == 0, hi, jnp.where(piece_row == 1, mid,
                                                      jnp.where(piece_row == 2, lo, 0.0)))
        rhs_scr[2 * HEAD_DIM:2 * HEAD_DIM + PIECE_ROWS, :] = blk.astype(BF16)
    w_rows = aux_ref[0:N_HEADS, :]
    q_pos = i * q_blk + lax.broadcasted_iota(I32, (1, q_blk), 1)
    key_lim = (q_pos // CHUNK + 1) * CHUNK
    row_iota = lax.broadcasted_iota(I32, (tk, q_blk), 0)

    def score_tile(t, _):
        off = pl.multiple_of(t * tk, tk)
        s = jnp.dot(kk_ref[pl.ds(off, tk), 0:LANES], qi_ext,
                    preferred_element_type=F32)
        acc = jnp.zeros((tk, q_blk), F32)
        for h in range(N_HEADS):
            acc = acc + w_rows[h:h + 1, :] * jnp.maximum(s[:, h * q_blk:(h + 1) * q_blk], 0.0)
        bits = pltpu.bitcast(acc, I32)
        key = bits ^ ((bits >> 31) & 0x7FFFFFFF)
        keys_scr[pl.ds(off, tk), :] = jnp.where(off + row_iota < key_lim, key, INT_MIN)
        return 0

    lax.fori_loop(0, n_tiles, score_tile, 0)

    count_iota = lax.broadcasted_iota(I32, (COUNT_ROWS, q_blk), 0)

    def count(pred):
        def tile(t, acc):
            off = pl.multiple_of(t * tk, tk)
            for r in range(tk // COUNT_ROWS):
                start = off + r * COUNT_ROWS
                acc = acc + pred(keys_scr[pl.ds(start, COUNT_ROWS), :], start + count_iota)
            return acc
        acc = lax.fori_loop(0, n_tiles, tile, jnp.zeros((COUNT_ROWS, q_blk), I32))
        return _col_reduce(acc, jnp.sum)

    def radix_pass(b, carry):
        prefix, cnt_ge = carry
        cand = prefix | jnp.left_shift(jnp.int32(1), 31 - b)
        cand_s = cand ^ INT_MIN
        cnt = count(lambda k, _: jnp.where(k >= cand_s, 1, 0))
        ok = cnt >= topk
        return jnp.where(ok, cand, prefix), jnp.where(ok, cnt, cnt_ge)

    total = jnp.zeros((1, q_blk), I32) + n_tiles * tk
    prefix, cnt_ge = lax.fori_loop(0, 32, radix_pass, (jnp.zeros((1, q_blk), I32), total))
    kth = prefix ^ INT_MIN
    thr = jnp.maximum(kth, MIN_VALID_KEY)

    has_tie = jnp.where(cnt_ge > topk, jnp.where(kth >= MIN_VALID_KEY, 1, 0), 0)

    @pl.when(jnp.max(has_tie) > 0)
    def _():
        n_gt = count(lambda k, _: jnp.where(k > thr, 1, 0))
        need = topk - n_gt

        def idx_pass(b, pre):
            cand = pre | jnp.left_shift(jnp.int32(1), idx_bits - 1 - b)
            c = count(lambda k, kidx: jnp.where(k == thr, jnp.where(kidx < cand, 1, 0), 0))
            return jnp.where(c < need, cand, pre)

        last = lax.fori_loop(0, idx_bits, idx_pass, jnp.zeros((1, q_blk), I32))

        def drop_tile(t, _):
            off = pl.multiple_of(t * tk, tk)
            k = keys_scr[pl.ds(off, tk), :]
            drop = jnp.where(k == thr, jnp.where(off + row_iota > last, INT_MIN, k), k)
            keys_scr[pl.ds(off, tk), :] = drop
            return 0

        lax.fori_loop(0, n_tiles, drop_tile, 0)

    acc_scr[...] = jnp.zeros_like(acc_scr)

    heads = [slice(h * q_blk, (h + 1) * q_blk) for h in range(N_HEADS)]

    def lane_row(x8):
        return jnp.concatenate([x8[h:h + 1, :] for h in range(N_HEADS)], axis=1)

    def tile_operands(t):
        off = pl.multiple_of(t * tk, tk)
        bias = jnp.where(keys_scr[pl.ds(off, tk), :] >= thr, 0.0, NEG_INF)
        return kk_ref[pl.ds(off, tk), :], bias, vat_ref[:, pl.ds(off, tk)]

    def attn_tile(t, m_all):
        kt, bias, vt = tile_operands(t)
        set_row_term(jnp.zeros((1, width), F32))
        logits = jnp.dot(kt, rhs_scr[...], preferred_element_type=F32)
        m_rows = []
        for h in range(N_HEADS):
            s = logits[:, heads[h]] + bias
            m_new = jnp.maximum(m_all[h:h + 1, :], _col_reduce(s, jnp.max))
            p_scr[:, heads[h]] = jnp.exp2(s - m_new).astype(BF16)
            m_rows.append(m_new)
        m_new = jnp.concatenate(m_rows, axis=0)
        pv = jnp.dot(vt, p_scr[...], preferred_element_type=F32)
        acc_scr[...] = acc_scr[...] * lane_row(jnp.exp2(m_all - m_new)) + pv
        return m_new

    def one_pass_tile(t, carry):
        m_ref, worst = carry
        kt, bias, vt = tile_operands(t)
        set_row_term(-lane_row(m_ref))
        x_all = jnp.dot(kt, rhs_scr[...], preferred_element_type=F32)
        tmax = []
        for h in range(N_HEADS):
            x = x_all[:, heads[h]] + bias
            p_scr[:, heads[h]] = jnp.exp2(x).astype(BF16)
            tmax.append(_col_reduce(x, jnp.max))
        tmax = jnp.concatenate(tmax, axis=0)
        up = jnp.maximum(tmax, 0.0)
        pv = jnp.dot(vt, p_scr[...], preferred_element_type=F32)
        acc_scr[...] = (acc_scr[...] + pv) * lane_row(jnp.exp2(-up))
        return m_ref + up, jnp.maximum(worst, tmax)

    m0 = jnp.full((N_HEADS, q_blk), M_INIT, F32)
    m1 = attn_tile(0, m0)
    _, worst = lax.fori_loop(1, n_tiles, one_pass_tile, (m1, m0))

    @pl.when(jnp.max(worst) > MAX_EXP_ARG)
    def _():
        acc_scr[...] = jnp.zeros_like(acc_scr)
        lax.fori_loop(0, n_tiles, attn_tile, m0)

    denom = acc_scr[LANE_ONES:LANE_ONES + 1, :]
    for pair in range(N_HEADS // 2):
        halves = []
        for h in (2 * pair, 2 * pair + 1):
            halves.append(acc_scr[0:HEAD_DIM, heads[h]] / denom[:, heads[h]])
        o_ref[:, pair * LANES:(pair + 1) * LANES] = (
            jnp.concatenate(halves, axis=0).T.astype(BF16))


def _dsa_call(qit, qat, auxt, kk, vat, *, q_blk, tk):
    B, _, _, S = qit.shape
    topk = min(TOPK_MAX, S // 4)
    idx_bits = max(1, int(np.ceil(np.log2(S))))
    width = N_HEADS * q_blk
    hm_spec = pl.BlockSpec((None, N_HEADS, HEAD_DIM, q_blk), lambda b, i: (b, 0, 0, i))
    return pl.pallas_call(
        functools.partial(_dsa_kernel, q_blk=q_blk, tk=tk, topk=topk, idx_bits=idx_bits),
        grid=(B, S // q_blk),
        in_specs=[
            hm_spec, hm_spec,
            pl.BlockSpec((None, 2 * N_HEADS, q_blk), lambda b, i: (b, 0, i)),
            pl.BlockSpec((None, S, 2 * LANES), lambda b, i: (b, 0, 0)),
            pl.BlockSpec((None, LANES, S), lambda b, i: (b, 0, 0)),
        ],
        out_specs=pl.BlockSpec((None, q_blk, W_HEADS), lambda b, i: (b, i, 0)),
        out_shape=jax.ShapeDtypeStruct((B, S, W_HEADS), BF16),
        scratch_shapes=[
            pltpu.VMEM((S, q_blk), I32),
            pltpu.VMEM((LANES, width), F32),
            pltpu.VMEM((tk, width), BF16),
            pltpu.VMEM((2 * LANES, width), BF16),
        ],
        compiler_params=_params(2),
        name="dsa",
    )(qit, qat, auxt, kk, vat)


def _fox_kernel(qi_tbl, kj_tbl, qt_ref, k_ref, vt_ref, auxq_ref, auxk_ref, neg_ref, o_ref,
                m_scr, acc_scr, pv_scr, done_scr, *, t):
    step = pl.program_id(1)
    i = qi_tbl[step]
    j = kj_tbl[step]
    top = lax.broadcasted_iota(I32, (LANES, t), 0) < HEAD_DIM
    piece_row = lax.broadcasted_iota(I32, (PIECE_ROWS, t), 0)
    ones_rows = jnp.ones((PIECE_ROWS, t), BF16)

    @pl.when(j == 0)
    def _():
        m_scr[...] = jnp.minimum(auxq_ref[N_HEADS:2 * N_HEADS, :]
                                 - auxk_ref[N_HEADS:2 * N_HEADS, t - 1:t], 0.0)
        acc_scr[...] = jnp.zeros_like(acc_scr)

    def head_operands(h, r):
        pair, e = divmod(h, 2)
        qp = qt_ref[pair * LANES:(pair + 1) * LANES, :]
        qm = jnp.where(top, qp, 0) if e == 0 else jnp.where(top, 0, qp)
        hi = r.astype(BF16).astype(F32)
        mid = (r - hi).astype(BF16).astype(F32)
        lo = r - hi - mid
        pieces = jnp.where(piece_row == 0, hi, jnp.where(piece_row == 1, mid,
                                                         jnp.where(piece_row == 2, lo, 0.0)))
        rhs = jnp.concatenate([qm, pieces.astype(BF16), neg_ref[h]], axis=0)
        v_ext = jnp.concatenate([vt_ref[h * HEAD_DIM:(h + 1) * HEAD_DIM, :], ones_rows], axis=0)
        return k_ref[:, 2 * pair * LANES:(2 * pair + 2) * LANES], rhs, v_ext

    def two_pass_body():
        causal = (j * t + lax.broadcasted_iota(I32, (t, t), 0)
                  <= i * t + lax.broadcasted_iota(I32, (t, t), 1))
        for h in range(N_HEADS):
            lhs, rhs, v_ext = head_operands(h, auxq_ref[N_HEADS + h:N_HEADS + h + 1, :])
            s = jnp.dot(lhs, rhs, preferred_element_type=F32)
            s = jnp.where(causal, s, NEG_INF)
            m_prev = jnp.where(j == 0, M_INIT, m_scr[h:h + 1, :])
            m_new = jnp.maximum(m_prev, _col_reduce(s, jnp.max))
            p = jnp.exp2(s - m_new).astype(BF16)
            m_scr[h:h + 1, :] = m_new
            acc_scr[h] = (acc_scr[h] * jnp.exp2(m_prev - m_new)
                          + jnp.dot(v_ext, p, preferred_element_type=F32))

    def one_pass_body(diag):
        if diag:
            causal = (lax.broadcasted_iota(I32, (t, t), 0)
                      <= lax.broadcasted_iota(I32, (t, t), 1))
        tmax, shifts = [], []
        for pair in range(N_HEADS // 2):
            ops = []
            for h in (2 * pair, 2 * pair + 1):
                cq = auxq_ref[N_HEADS + h:N_HEADS + h + 1, :]
                ck_first = auxk_ref[N_HEADS + h:N_HEADS + h + 1, 0:1]
                ck_last = auxk_ref[N_HEADS + h:N_HEADS + h + 1, t - 1:t]
                shift = jnp.minimum(cq - ck_last, 0.0) - jnp.minimum(cq - ck_first, 0.0)
                shift = jnp.where(j == 0, 0.0, shift)
                shifts.append(shift)
                ops.append(head_operands(h, cq - (m_scr[h:h + 1, :] + shift)))
            x2 = jnp.dot(ops[0][0], jnp.concatenate([ops[0][1], ops[1][1]], axis=1),
                         preferred_element_type=F32)
            for e in range(2):
                x = x2[:, e * t:(e + 1) * t]
                if diag:
                    x = jnp.where(causal, x, NEG_INF)
                tmax.append(_col_reduce(x, jnp.max))
                pv_scr[2 * pair + e] = jnp.dot(ops[e][2], jnp.exp2(x).astype(BF16),
                                               preferred_element_type=F32)
        tmax_all = jnp.concatenate(tmax, axis=0)
        ok = jnp.logical_and(jnp.max(tmax_all) <= MAX_EXP_ARG, jnp.min(tmax_all) >= -MAX_EXP_ARG)

        @pl.when(ok)
        def _():
            for h in range(N_HEADS):
                up = jnp.maximum(tmax[h], 0.0)
                m_scr[h:h + 1, :] = m_scr[h:h + 1, :] + shifts[h] + up
                acc_scr[h] = (acc_scr[h] * jnp.exp2(-shifts[h]) + pv_scr[h]) * jnp.exp2(-up)

        return jnp.where(ok, 1, 0)

    done_scr[0] = 0

    @pl.when(j < i)
    def _():
        done_scr[0] = one_pass_body(False)

    @pl.when(j == i)
    def _():
        done_scr[0] = one_pass_body(True)

    @pl.when(done_scr[0] == 0)
    def _():
        two_pass_body()

    @pl.when(j == i)
    def _():
        for pair in range(N_HEADS // 2):
            halves = [acc_scr[h, 0:HEAD_DIM, :] / acc_scr[h, HEAD_DIM:HEAD_DIM + 1, :]
                      for h in (2 * pair, 2 * pair + 1)]
            o_ref[:, pair * LANES:(pair + 1) * LANES] = (
                jnp.concatenate(halves, axis=0).T.astype(BF16))


def _fox_call(qt, k, vt, auxt, *, t):
    B, S, _ = k.shape
    n = S // t
    neg = np.zeros((N_HEADS, LANES - PIECE_ROWS, t), np.float32)
    for h in range(N_HEADS):
        neg[h, N_PIECES * h:N_PIECES * (h + 1), :] = -1.0
    qi_tbl = np.concatenate([np.full(i + 1, i, np.int32) for i in range(n)])
    kj_tbl = np.concatenate([np.arange(i + 1, dtype=np.int32) for i in range(n)])
    grid_spec = pltpu.PrefetchScalarGridSpec(
        num_scalar_prefetch=2,
        grid=(B, len(qi_tbl)),
        in_specs=[
            pl.BlockSpec((None, W_HEADS, t), lambda b, s, qi, kj: (b, 0, qi[s])),
            pl.BlockSpec((None, t, 2 * W_HEADS), lambda b, s, qi, kj: (b, kj[s], 0)),
            pl.BlockSpec((None, W_HEADS, t), lambda b, s, qi, kj: (b, 0, kj[s])),
            pl.BlockSpec((None, 2 * N_HEADS, t), lambda b, s, qi, kj: (b, 0, qi[s])),
            pl.BlockSpec((None, 2 * N_HEADS, t), lambda b, s, qi, kj: (b, 0, kj[s])),
            pl.BlockSpec((N_HEADS, LANES - PIECE_ROWS, t), lambda b, s, qi, kj: (0, 0, 0)),
        ],
        out_specs=pl.BlockSpec((None, t, W_HEADS), lambda b, s, qi, kj: (b, qi[s], 0)),
        scratch_shapes=[
            pltpu.VMEM((N_HEADS, t), F32),
            pltpu.VMEM((N_HEADS, HEAD_DIM + PIECE_ROWS, t), F32),
            pltpu.VMEM((N_HEADS, HEAD_DIM + PIECE_ROWS, t), F32),
            pltpu.SMEM((1,), I32),
        ],
    )
    return pl.pallas_call(
        functools.partial(_fox_kernel, t=t),
        grid_spec=grid_spec,
        out_shape=jax.ShapeDtypeStruct((B, S, W_HEADS), BF16),
        compiler_params=_params(2),
        name="fox",
    )(jnp.asarray(qi_tbl), jnp.asarray(kj_tbl), qt, k, vt, auxt, auxt, jnp.asarray(neg, BF16))


def _merge_kernel(oa_ref, ob_ref, ga_ref, gb_ref, x_ref, g1_ref, sc_ref, sh_ref, n2_ref,
                  wa_ref, wb_ref, wo_ref, x1_ref, h2_ref):
    ya = jnp.dot(oa_ref[...], wa_ref[...], preferred_element_type=F32)
    yb = jnp.dot(ob_ref[...], wb_ref[...], preferred_element_type=F32)
    mix = ga_ref[...].astype(F32) * ya + gb_ref[...].astype(F32) * yb
    y = jnp.dot(mix.astype(BF16), wo_ref[...], preferred_element_type=F32)
    x1 = x_ref[...] + g1_ref[...] * y
    x1_ref[...] = x1
    h2_ref[...] = _rms_mod(x1, n2_ref[...], sc_ref[...], sh_ref[...]).astype(BF16)


def _merge_call(oa, ob, ga, gb, x, g1, sc2, sh2, n2, wa, wb, wo, *, tm):
    B, S, _ = x.shape
    row = lambda b, s: (b, s, 0)
    const2 = lambda b, s: (0, 0)
    vec = pl.BlockSpec((None, 1, D_MODEL), lambda b, s: (b, 0, 0))
    return pl.pallas_call(
        _merge_kernel,
        grid=(B, S // tm),
        in_specs=[
            pl.BlockSpec((None, tm, W_HEADS), row),
            pl.BlockSpec((None, tm, W_HEADS), row),
            pl.BlockSpec((None, tm, D_MODEL), row),
            pl.BlockSpec((None, tm, D_MODEL), row),
            pl.BlockSpec((None, tm, D_MODEL), row),
            vec, vec, vec,
            pl.BlockSpec((1, D_MODEL), const2),
            pl.BlockSpec((W_HEADS, D_MODEL), const2),
            pl.BlockSpec((W_HEADS, D_MODEL), const2),
            pl.BlockSpec((D_MODEL, D_MODEL), const2),
        ],
        out_specs=[pl.BlockSpec((None, tm, D_MODEL), row),
                   pl.BlockSpec((None, tm, D_MODEL), row)],
        out_shape=[jax.ShapeDtypeStruct((B, S, D_MODEL), F32),
                   jax.ShapeDtypeStruct((B, S, D_MODEL), BF16)],
        compiler_params=_params(2),
        name="merge",
    )(oa, ob, ga, gb, x, g1, sc2, sh2, n2, wa, wb, wo)


def _ffn_kernel(h_ref, x_ref, g2_ref, wup_ref, cw_ref, cb_ref, wdn_ref, fg_ref, o_ref,
                carry_scr, *, tm, fc, final):
    s_idx = pl.program_id(1)

    @pl.when(s_idx == 0)
    def _():
        carry_scr[...] = jnp.zeros_like(carry_scr)

    hb = h_ref[...]
    row = lax.broadcasted_iota(I32, (SUBLANES, fc), 0)
    acc = jnp.zeros((tm, D_MODEL), F32)
    for cblk in range(D_FF // fc):
        halves = []
        for part in range(2):
            col = part * D_FF + cblk * fc
            u = jnp.dot(hb, wup_ref[:, col:col + fc], preferred_element_type=F32)
            prev = carry_scr[:, col:col + fc]
            r1 = pltpu.roll(u, 1, 0)
            r2 = pltpu.roll(u, 2, 0)
            head1 = jnp.where(row == 0, prev[7:8], r1[:SUBLANES])
            head2 = jnp.where(row == 0, prev[6:7], jnp.where(row == 1, prev[7:8], r2[:SUBLANES]))
            u1 = jnp.concatenate([head1, r1[SUBLANES:]], axis=0)
            u2 = jnp.concatenate([head2, r2[SUBLANES:]], axis=0)
            carry_scr[:, col:col + fc] = u[tm - SUBLANES:tm]
            cw = cw_ref[:, col:col + fc]
            halves.append(cb_ref[:, col:col + fc] + (cw[0:1] * u2 + cw[1:2] * u1 + cw[2:3] * u))
        act = (jax.nn.silu(halves[0]) * halves[1]).astype(BF16)
        acc = acc + jnp.dot(act, wdn_ref[cblk * fc:(cblk + 1) * fc, :],
                            preferred_element_type=F32)
    x2 = x_ref[...] + g2_ref[...] * acc
    if final:
        var = jnp.mean(x2 * x2, axis=-1, keepdims=True)
        x2 = x2 * lax.rsqrt(var + EPS) * fg_ref[...]
    o_ref[...] = x2


def _ffn_call(h2, x1, g2, wup, cw, cb, wdn, fg, *, tm, fc, final):
    B, S, _ = x1.shape
    row = lambda b, s: (b, s, 0)
    const2 = lambda b, s: (0, 0)
    return pl.pallas_call(
        functools.partial(_ffn_kernel, tm=tm, fc=fc, final=final),
        grid=(B, S // tm),
        in_specs=[
            pl.BlockSpec((None, tm, D_MODEL), row),
            pl.BlockSpec((None, tm, D_MODEL), row),
            pl.BlockSpec((None, 1, D_MODEL), lambda b, s: (b, 0, 0)),
            pl.BlockSpec((D_MODEL, 2 * D_FF), const2, pipeline_mode=pl.Buffered(1)),
            pl.BlockSpec((SUBLANES, 2 * D_FF), const2),
            pl.BlockSpec((1, 2 * D_FF), const2),
            pl.BlockSpec((D_FF, D_MODEL), const2, pipeline_mode=pl.Buffered(1)),
            pl.BlockSpec((1, D_MODEL), const2),
        ],
        out_specs=pl.BlockSpec((None, tm, D_MODEL), row),
        out_shape=jax.ShapeDtypeStruct((B, S, D_MODEL), F32),
        scratch_shapes=[pltpu.VMEM((SUBLANES, 2 * D_FF), F32)],
        compiler_params=_params(2),
        name="ffn",
    )(h2, x1, g2, wup, cw, cb, wdn, fg)


def _reorder_w_in(w):
    o = np.cumsum([0, W_HEADS, HEAD_DIM, HEAD_DIM, W_HEADS, HEAD_DIM, N_HEADS,
                   W_HEADS, W_HEADS, W_HEADS, N_HEADS, 2 * D_MODEL])
    seg = lambda k: w[:, o[k]:o[k + 1]]
    q_a, k_a, v_a, q_i, k_i, w_i, q_f, k_f, v_f, f_f, gates = [seg(k) for k in range(11)]
    pad = jnp.zeros((w.shape[0], N_COLS - int(o[-1])), w.dtype)
    return jnp.concatenate([q_a, q_i, q_f, k_f, v_f, gates, k_i, k_a, v_a, w_i, f_f, pad],
                           axis=1).astype(BF16)


def kernel(x, c, positions, mod_w, mod_b, norm1_g, norm2_g, w_in, forget_bias, w_branch_a,
           w_branch_b, w_out, w_up, conv_w, conv_b, w_down, final_g):
    B, S, _ = x.shape
    depth = mod_w.shape[0]
    tm = min(TOKEN_TILE, S)
    mod = _mod_call(c, mod_w, mod_b)[:, :, :B].reshape(depth, 6, B, 1, D_MODEL)
    pos = positions.astype(F32).reshape(B, S, 1)

    inv_freq = ROPE_THETA ** (-jnp.arange(0, ROT_DIM, 2, dtype=F32) / ROT_DIM)
    l64 = np.arange(LANES) % HEAD_DIM
    invf = jnp.where(l64 < ROT_DIM, inv_freq[l64 % (ROT_DIM // 2)], 0.0).reshape(1, LANES)

    place = np.zeros((N_PIECES, LANES, LANES), np.float32)
    for h in range(N_HEADS):
        for piece in range(N_PIECES):
            place[piece, LANE_FORGET + h, LANE_CK + N_PIECES * h + piece] = 1.0
    place = jnp.asarray(place, BF16)

    for l in range(depth):
        sh1, sc1, g1, sh2, sc2, g2 = [mod[l, k] for k in range(6)]
        fb = jnp.zeros((1, LANES), F32).at[0, LANE_FORGET:LANE_FORGET + N_HEADS].set(forget_bias[l])
        (qat, qit, qft, kf, vft, ga, gb, kk, vat, auxt) = _in_call(
            x, sh1, sc1, norm1_g[l].reshape(1, D_MODEL), pos, invf, fb, place,
            _reorder_w_in(w_in[l]), tm=tm)
        oa = _dsa_call(qit, qat, auxt, kk, vat, q_blk=DSA_Q_BLOCK, tk=min(DSA_KEY_TILE, S))
        ob = _fox_call(qft, kf, vft, auxt, t=tm)
        x1, h2 = _merge_call(oa, ob, ga, gb, x, g1, sc2, sh2, norm2_g[l].reshape(1, D_MODEL),
                             w_branch_a[l].astype(BF16), w_branch_b[l].astype(BF16),
                             w_out[l].astype(BF16), tm=tm)
        cw8 = jnp.zeros((SUBLANES, 2 * D_FF), F32).at[:conv_w.shape[1]].set(conv_w[l])
        x = _ffn_call(h2, x1, g2, w_up[l].astype(BF16), cw8, conv_b[l].reshape(1, 2 * D_FF),
                      w_down[l].astype(BF16), final_g.reshape(1, D_MODEL),
                      tm=tm, fc=FFN_CHUNK, final=(l == depth - 1))
    return x
```

```python
import functools

import numpy as np
import jax
import jax.numpy as jnp
from jax import lax
from jax.experimental import pallas as pl
from jax.experimental.pallas import tpu as pltpu

F32 = jnp.float32
BF16 = jnp.bfloat16
I32 = jnp.int32

D_MODEL = 1024
CHUNK = 64
HEAD_DIM = 64
N_HEADS = 8
TOPK_MAX = 256
ROPE_THETA = 500000.0
ROT_DIM = HEAD_DIM // 4
D_FF = 2816
EPS = 1e-6
NEG_INF = -1e30
W_HEADS = N_HEADS * HEAD_DIM

LANES = 128
SUBLANES = 8
VMEM_LIMIT = 56 * 1024 * 1024

TOKEN_TILE = 512
DSA_Q_BLOCK = 2 * LANES
DSA_KEY_TILE = 512
COUNT_ROWS = 64
RADIX_GROUP = 4
FFN_CHUNK = 2816

COL_QA = 0
COL_QI = 512
COL_QF = 1024
COL_KF = 1536
COL_VF = 2048
COL_GATES = 2560
COL_SMALL = 4608
N_COLS = 4864
LANE_WIDX = 64
LANE_FORGET = 72
LANE_ONES = 80
N_PIECES = 3
PIECE_ROWS = 16
LANE_CK = PIECE_ROWS

INT_MIN = -(2 ** 31)
M_INIT = -3.0e38
LOG2E = 1.4426950408889634
MAX_EXP_ARG = 100.0


def _f32_key(v):
    b = int(np.float32(v).view(np.int32))
    return b ^ ((b >> 31) & 0x7FFFFFFF)


MIN_VALID_KEY = _f32_key(0.5 * NEG_INF) + 1


def _params(n_grid):
    return pltpu.CompilerParams(
        dimension_semantics=("arbitrary",) * n_grid, vmem_limit_bytes=VMEM_LIMIT)


def _col_reduce(x, op):
    rows = x.shape[0]
    part = op(x.reshape(rows // SUBLANES, SUBLANES, x.shape[1]), axis=0)
    return op(part, axis=0, keepdims=True)


def _mod_kernel(c_ref, w_ref, b_ref, o_ref):
    o_ref[...] = jnp.dot(c_ref[...], w_ref[...], precision=lax.Precision.HIGHEST,
                         preferred_element_type=F32) + b_ref[...]


def _mod_call(c, mod_w, mod_b):
    depth = mod_w.shape[0]
    c8 = jnp.zeros((SUBLANES, D_MODEL), F32).at[: c.shape[0]].set(c)
    b4 = mod_b.reshape(depth, 6, 1, D_MODEL)
    return pl.pallas_call(
        _mod_kernel,
        grid=(depth, 6),
        in_specs=[
            pl.BlockSpec((SUBLANES, D_MODEL), lambda l, j: (0, 0)),
            pl.BlockSpec((None, D_MODEL, D_MODEL), lambda l, j: (l, 0, j)),
            pl.BlockSpec((None, None, 1, D_MODEL), lambda l, j: (l, j, 0, 0)),
        ],
        out_specs=pl.BlockSpec((None, None, SUBLANES, D_MODEL), lambda l, j: (l, j, 0, 0)),
        out_shape=jax.ShapeDtypeStruct((depth, 6, SUBLANES, D_MODEL), F32),
        compiler_params=_params(2),
        name="mod",
    )(c8, mod_w, b4)


def _rms_mod(x, gain, scale, shift):
    var = jnp.mean(x * x, axis=-1, keepdims=True)
    y = x * lax.rsqrt(var + EPS) * gain
    return y * (1.0 + scale) + shift


def _split3(x):
    p0 = x.astype(BF16)
    r1 = x - p0.astype(F32)
    p1 = r1.astype(BF16)
    return p0, p1, (r1 - p1.astype(F32)).astype(BF16)


def _in_kernel(x_ref, sh_ref, sc_ref, g_ref, pos_ref, invf_ref, fb_ref, place_ref, w_ref,
               qat_ref, qit_ref, qft_ref, kf_ref, vft_ref, ga_ref, gb_ref,
               kk_ref, vat_ref, auxt_ref, carry_scr, *, tm):
    s_idx = pl.program_id(1)

    @pl.when(s_idx == 0)
    def _():
        carry_scr[...] = jnp.zeros_like(carry_scr)

    h = _rms_mod(x_ref[...], g_ref[...], sc_ref[...], sh_ref[...])
    hb = h.astype(BF16)

    ang = pos_ref[...] * invf_ref[...]
    cos = jnp.cos(ang)
    sin = jnp.sin(ang)
    l64 = lax.broadcasted_iota(I32, (tm, LANES), 1) & (HEAD_DIM - 1)
    half = ROT_DIM // 2
    t_cos = jnp.where(l64 < ROT_DIM, cos, 1.0)
    t_lo = jnp.where(l64 < half, -sin, 0.0)
    t_hi = jnp.where(l64 < half, 0.0, jnp.where(l64 < ROT_DIM, sin, 0.0))

    def rope(xc):
        return (xc * t_cos + pltpu.roll(xc, LANES - half, 1) * t_lo
                + pltpu.roll(xc, half, 1) * t_hi)

    def proj(col, width):
        return jnp.dot(hb, w_ref[:, col:col + width], preferred_element_type=F32)

    for col, out_ref, scale in ((COL_QA, qat_ref, 0.125 * LOG2E), (COL_QI, qit_ref, 0.125)):
        p = proj(col, W_HEADS)
        for cblk in range(W_HEADS // LANES):
            r = (rope(p[:, cblk * LANES:(cblk + 1) * LANES]) * scale).T.astype(BF16)
            out_ref[2 * cblk] = r[:HEAD_DIM]
            out_ref[2 * cblk + 1] = r[HEAD_DIM:]

    for col, out_ref, scale in ((COL_QF, qft_ref, 0.125 * LOG2E), (COL_VF, vft_ref, 1.0)):
        p = proj(col, W_HEADS)
        for cblk in range(W_HEADS // LANES):
            sl = slice(cblk * LANES, (cblk + 1) * LANES)
            out_ref[sl, :] = (p[:, sl] * scale).T.astype(BF16)
    ga_ref[...] = jax.nn.sigmoid(proj(COL_GATES, D_MODEL)).astype(BF16)
    gb_ref[...] = jax.nn.sigmoid(proj(COL_GATES + D_MODEL, D_MODEL)).astype(BF16)

    lane = lax.broadcasted_iota(I32, (tm, LANES), 1)
    ones_lanes = jnp.where(lane < N_PIECES, 1.0, 0.0)
    small = proj(COL_SMALL, 2 * LANES)
    kk_ref[...] = jnp.concatenate([rope(small[:, :LANES]), ones_lanes], axis=1).astype(BF16)
    blk = small[:, LANES:]

    is_f = (lane >= LANE_FORGET) & (lane < LANE_FORGET + N_HEADS)
    z = blk + fb_ref[...]
    logf = jnp.minimum(z, 0.0) - jnp.log1p(jnp.exp(-jnp.abs(z)))
    logf = jnp.where(is_f, logf, 0.0)
    tri = (lax.broadcasted_iota(I32, (tm, tm), 0)
           >= lax.broadcasted_iota(I32, (tm, tm), 1)).astype(BF16)
    p0, p1, p2 = _split3(logf)
    cum = (jnp.dot(tri, p0, preferred_element_type=F32)
           + jnp.dot(tri, p1, preferred_element_type=F32)
           + jnp.dot(tri, p2, preferred_element_type=F32)) + carry_scr[0:1, :]
    carry_scr[0:1, :] = cum[tm - 1:tm, :]
    cum = cum * LOG2E

    c0, c1, c2 = _split3(cum)
    kaux = (jnp.dot(c0, place_ref[0], preferred_element_type=F32)
            + jnp.dot(c1, place_ref[1], preferred_element_type=F32)
            + jnp.dot(c2, place_ref[2], preferred_element_type=F32) + ones_lanes).astype(BF16)
    kf = proj(COL_KF, W_HEADS).astype(BF16)
    for pair in range(N_HEADS // 2):
        kf_ref[:, 2 * pair * LANES:(2 * pair + 1) * LANES] = kf[:, pair * LANES:(pair + 1) * LANES]
        kf_ref[:, (2 * pair + 1) * LANES:(2 * pair + 2) * LANES] = kaux

    is_w = (lane >= LANE_WIDX) & (lane < LANE_WIDX + N_HEADS)
    comb = jnp.where(is_w, blk * (N_HEADS ** -0.5), jnp.where(is_f, cum, blk))
    comb_t = jnp.where(lane == LANE_ONES, 1.0, comb).T
    vat_ref[...] = comb_t.astype(BF16)
    auxt_ref[...] = comb_t[LANE_WIDX:LANE_WIDX + 2 * N_HEADS]


def _in_call(x, sh, sc, gain, pos, invf, fb, place, w_all, *, tm):
    B, S, _ = x.shape
    hm_t = jax.ShapeDtypeStruct((B, N_HEADS, HEAD_DIM, S), BF16)
    wide_t = jax.ShapeDtypeStruct((B, W_HEADS, S), BF16)
    gate = jax.ShapeDtypeStruct((B, S, D_MODEL), BF16)
    row = lambda b, s: (b, s, 0)
    col = lambda b, s: (b, 0, s)
    vec = pl.BlockSpec((None, 1, D_MODEL), lambda b, s: (b, 0, 0))
    const2 = lambda b, s: (0, 0)
    hm_spec = pl.BlockSpec((None, N_HEADS, HEAD_DIM, tm), lambda b, s: (b, 0, 0, s))
    return pl.pallas_call(
        functools.partial(_in_kernel, tm=tm),
        grid=(B, S // tm),
        in_specs=[
            pl.BlockSpec((None, tm, D_MODEL), row),
            vec, vec,
            pl.BlockSpec((1, D_MODEL), const2),
            pl.BlockSpec((None, tm, 1), row),
            pl.BlockSpec((1, LANES), const2),
            pl.BlockSpec((1, LANES), const2),
            pl.BlockSpec((N_PIECES, LANES, LANES), lambda b, s: (0, 0, 0)),
            pl.BlockSpec((D_MODEL, N_COLS), const2, pipeline_mode=pl.Buffered(1)),
        ],
        out_specs=[
            hm_spec, hm_spec,
            pl.BlockSpec((None, W_HEADS, tm), col),
            pl.BlockSpec((None, tm, 2 * W_HEADS), row),
            pl.BlockSpec((None, W_HEADS, tm), col),
            pl.BlockSpec((None, tm, D_MODEL), row),
            pl.BlockSpec((None, tm, D_MODEL), row),
            pl.BlockSpec((None, tm, 2 * LANES), row),
            pl.BlockSpec((None, LANES, tm), col),
            pl.BlockSpec((None, 2 * N_HEADS, tm), col),
        ],
        out_shape=[hm_t, hm_t, wide_t,
                   jax.ShapeDtypeStruct((B, S, 2 * W_HEADS), BF16),
                   wide_t, gate, gate,
                   jax.ShapeDtypeStruct((B, S, 2 * LANES), BF16),
                   jax.ShapeDtypeStruct((B, LANES, S), BF16),
                   jax.ShapeDtypeStruct((B, 2 * N_HEADS, S), F32)],
        scratch_shapes=[pltpu.VMEM((SUBLANES, LANES), F32)],
        compiler_params=_params(2),
        name="in_proj",
    )(x, sh, sc, gain, pos, invf, fb, place, w_all)


def _dsa_kernel(qit_ref, qat_ref, aux_ref, kk_ref, vat_ref, o_ref,
                keys_scr, acc_scr, p_scr, rhs_scr, *, q_blk, tk, topk, idx_bits):
    i = pl.program_id(1)
    width = N_HEADS * q_blk
    n_tiles = ((i + 1) * q_blk + tk - 1) // tk

    zeros = jnp.zeros((HEAD_DIM, width), BF16)
    qi_ext = jnp.concatenate(
        [jnp.concatenate([qit_ref[h] for h in range(N_HEADS)], axis=1), zeros], axis=0)
    rhs_scr[...] = jnp.concatenate(
        [zeros, jnp.concatenate([qat_ref[h] for h in range(N_HEADS)], axis=1),
         jnp.zeros((LANES, width), BF16)], axis=0)
    piece_row = lax.broadcasted_iota(I32, (PIECE_ROWS, width), 0)

    def set_row_term(r):
        hi = r.astype(BF16).astype(F32)
        mid = (r - hi).astype(BF16).astype(F32)
        lo = r - hi - mid
        blk = jnp.where(piece_row == 0, hi, jnp.where(piece_row == 1, mid,
                                                      jnp.where(piece_row == 2, lo, 0.0)))
        rhs_scr[2 * HEAD_DIM:2 * HEAD_DIM + PIECE_ROWS, :] = blk.astype(BF16)
    w_rows = aux_ref[0:N_HEADS, :]
    q_pos = i * q_blk + lax.broadcasted_iota(I32, (1, q_blk), 1)
    key_lim = (q_pos // CHUNK + 1) * CHUNK
    row_iota = lax.broadcasted_iota(I32, (tk, q_blk), 0)

    def score_tile(t, _):
        off = pl.multiple_of(t * tk, tk)
        s = jnp.dot(kk_ref[pl.ds(off, tk), 0:LANES], qi_ext,
                    preferred_element_type=F32)
        acc = jnp.zeros((tk, q_blk), F32)
        for h in range(N_HEADS):
            acc = acc + w_rows[h:h + 1, :] * jnp.maximum(s[:, h * q_blk:(h + 1) * q_blk], 0.0)
        bits = pltpu.bitcast(acc, I32)
        key = bits ^ ((bits >> 31) & 0x7FFFFFFF)
        keys_scr[pl.ds(off, tk), :] = jnp.where(off + row_iota < key_lim, key, INT_MIN)
        return 0

    lax.fori_loop(0, n_tiles, score_tile, 0)

    count_iota = lax.broadcasted_iota(I32, (COUNT_ROWS, q_blk), 0)

    def count(pred):
        def tile(t, acc):
            off = pl.multiple_of(t * tk, tk)
            for r in range(tk // COUNT_ROWS):
                start = off + r * COUNT_ROWS
                acc = acc + pred(keys_scr[pl.ds(start, COUNT_ROWS), :], start + count_iota)
            return acc
        acc = lax.fori_loop(0, n_tiles, tile, jnp.zeros((COUNT_ROWS, q_blk), I32))
        return _col_reduce(acc, jnp.sum)

    def radix_pass(b, carry):
        prefix, cnt_ge = carry
        cand = prefix | jnp.left_shift(jnp.int32(1), 31 - b)
        cand_s = cand ^ INT_MIN
        cnt = count(lambda k, _: jnp.where(k >= cand_s, 1, 0))
        ok = cnt >= topk
        return jnp.where(ok, cand, prefix), jnp.where(ok, cnt, cnt_ge)

    def radix_group(state):
        g, prefix, cnt_ge, _ = state
        prefix, cnt_ge = lax.fori_loop(g * RADIX_GROUP, (g + 1) * RADIX_GROUP, radix_pass,
                                       (prefix, cnt_ge))
        return g + 1, prefix, cnt_ge, jnp.max(jnp.where(cnt_ge == topk, 0, 1))

    total = jnp.zeros((1, q_blk), I32) + n_tiles * tk
    _, prefix, cnt_ge, _ = lax.while_loop(
        lambda state: jnp.logical_and(state[0] < 32 // RADIX_GROUP, state[3] > 0), radix_group,
        (jnp.int32(0), jnp.zeros((1, q_blk), I32), total, jnp.int32(1)))
    kth = prefix ^ INT_MIN
    thr = jnp.maximum(kth, MIN_VALID_KEY)

    has_tie = jnp.where(cnt_ge > topk, jnp.where(kth >= MIN_VALID_KEY, 1, 0), 0)

    @pl.when(jnp.max(has_tie) > 0)
    def _():
        n_gt = count(lambda k, _: jnp.where(k > thr, 1, 0))
        need = topk - n_gt

        def idx_pass(b, pre):
            cand = pre | jnp.left_shift(jnp.int32(1), idx_bits - 1 - b)
            c = count(lambda k, kidx: jnp.where(k == thr, jnp.where(kidx < cand, 1, 0), 0))
            return jnp.where(c < need, cand, pre)

        last = lax.fori_loop(0, idx_bits, idx_pass, jnp.zeros((1, q_blk), I32))

        def drop_tile(t, _):
            off = pl.multiple_of(t * tk, tk)
            k = keys_scr[pl.ds(off, tk), :]
            drop = jnp.where(k == thr, jnp.where(off + row_iota > last, INT_MIN, k), k)
            keys_scr[pl.ds(off, tk), :] = drop
            return 0

        lax.fori_loop(0, n_tiles, drop_tile, 0)

    acc_scr[...] = jnp.zeros_like(acc_scr)

    heads = [slice(h * q_blk, (h + 1) * q_blk) for h in range(N_HEADS)]

    def lane_row(x8):
        return jnp.concatenate([x8[h:h + 1, :] for h in range(N_HEADS)], axis=1)

    def tile_operands(t):
        off = pl.multiple_of(t * tk, tk)
        bias = jnp.where(keys_scr[pl.ds(off, tk), :] >= thr, 0.0, NEG_INF)
        return kk_ref[pl.ds(off, tk), :], bias, vat_ref[:, pl.ds(off, tk)]

    def attn_tile(t, m_all):
        kt, bias, vt = tile_operands(t)
        set_row_term(jnp.zeros((1, width), F32))
        logits = jnp.dot(kt, rhs_scr[...], preferred_element_type=F32)
        m_rows = []
        for h in range(N_HEADS):
            s = logits[:, heads[h]] + bias
            m_new = jnp.maximum(m_all[h:h + 1, :], _col_reduce(s, jnp.max))
            p_scr[:, heads[h]] = jnp.exp2(s - m_new).astype(BF16)
            m_rows.append(m_new)
        m_new = jnp.concatenate(m_rows, axis=0)
        pv = jnp.dot(vt, p_scr[...], preferred_element_type=F32)
        acc_scr[...] = acc_scr[...] * lane_row(jnp.exp2(m_all - m_new)) + pv
        return m_new

    def one_pass_tile(t, carry):
        m_ref, worst = carry
        kt, bias, vt = tile_operands(t)
        set_row_term(-lane_row(m_ref))
        x_all = jnp.dot(kt, rhs_scr[...], preferred_element_type=F32)
        tmax = []
        for h in range(N_HEADS):
            x = x_all[:, heads[h]] + bias
            p_scr[:, heads[h]] = jnp.exp2(x).astype(BF16)
            tmax.append(_col_reduce(x, jnp.max))
        tmax = jnp.concatenate(tmax, axis=0)
        up = jnp.maximum(tmax, 0.0)
        pv = jnp.dot(vt, p_scr[...], preferred_element_type=F32)
        acc_scr[...] = (acc_scr[...] + pv) * lane_row(jnp.exp2(-up))
        return m_ref + up, jnp.maximum(worst, tmax)

    m0 = jnp.full((N_HEADS, q_blk), M_INIT, F32)
    m1 = attn_tile(0, m0)
    _, worst = lax.fori_loop(1, n_tiles, one_pass_tile, (m1, m0))

    @pl.when(jnp.max(worst) > MAX_EXP_ARG)
    def _():
        acc_scr[...] = jnp.zeros_like(acc_scr)
        lax.fori_loop(0, n_tiles, attn_tile, m0)

    denom = acc_scr[LANE_ONES:LANE_ONES + 1, :]
    for pair in range(N_HEADS // 2):
        halves = []
        for h in (2 * pair, 2 * pair + 1):
            halves.append(acc_scr[0:HEAD_DIM, heads[h]] / denom[:, heads[h]])
        o_ref[:, pair * LANES:(pair + 1) * LANES] = (
            jnp.concatenate(halves, axis=0).T.astype(BF16))


def _dsa_call(qit, qat, auxt, kk, vat, *, q_blk, tk):
    B, _, _, S = qit.shape
    topk = min(TOPK_MAX, S // 4)
    idx_bits = max(1, int(np.ceil(np.log2(S))))
    width = N_HEADS * q_blk
    hm_spec = pl.BlockSpec((None, N_HEADS, HEAD_DIM, q_blk), lambda b, i: (b, 0, 0, i))
    return pl.pallas_call(
        functools.partial(_dsa_kernel, q_blk=q_blk, tk=tk, topk=topk, idx_bits=idx_bits),
        grid=(B, S // q_blk),
        in_specs=[
            hm_spec, hm_spec,
            pl.BlockSpec((None, 2 * N_HEADS, q_blk), lambda b, i: (b, 0, i)),
            pl.BlockSpec((None, S, 2 * LANES), lambda b, i: (b, 0, 0)),
            pl.BlockSpec((None, LANES, S), lambda b, i: (b, 0, 0)),
        ],
        out_specs=pl.BlockSpec((None, q_blk, W_HEADS), lambda b, i: (b, i, 0)),
        out_shape=jax.ShapeDtypeStruct((B, S, W_HEADS), BF16),
        scratch_shapes=[
            pltpu.VMEM((S, q_blk), I32),
            pltpu.VMEM((LANES, width), F32),
            pltpu.VMEM((tk, width), BF16),
            pltpu.VMEM((2 * LANES, width), BF16),
        ],
        compiler_params=_params(2),
        name="dsa",
    )(qit, qat, auxt, kk, vat)


def _fox_kernel(qi_tbl, kj_tbl, qt_ref, k_ref, vt_ref, auxq_ref, auxk_ref, neg_ref, o_ref,
                m_scr, acc_scr, pv_scr, done_scr, *, t):
    step = pl.program_id(1)
    i = qi_tbl[step]
    j = kj_tbl[step]
    top = lax.broadcasted_iota(I32, (LANES, t), 0) < HEAD_DIM
    piece_row = lax.broadcasted_iota(I32, (PIECE_ROWS, t), 0)
    ones_rows = jnp.ones((PIECE_ROWS, t), BF16)

    @pl.when(j == 0)
    def _():
        m_scr[...] = jnp.minimum(auxq_ref[N_HEADS:2 * N_HEADS, :]
                                 - auxk_ref[N_HEADS:2 * N_HEADS, t - 1:t], 0.0)
        acc_scr[...] = jnp.zeros_like(acc_scr)

    def head_operands(h, r):
        pair, e = divmod(h, 2)
        qp = qt_ref[pair * LANES:(pair + 1) * LANES, :]
        qm = jnp.where(top, qp, 0) if e == 0 else jnp.where(top, 0, qp)
        hi = r.astype(BF16).astype(F32)
        mid = (r - hi).astype(BF16).astype(F32)
        lo = r - hi - mid
        pieces = jnp.where(piece_row == 0, hi, jnp.where(piece_row == 1, mid,
                                                         jnp.where(piece_row == 2, lo, 0.0)))
        rhs = jnp.concatenate([qm, pieces.astype(BF16), neg_ref[h]], axis=0)
        v_ext = jnp.concatenate([vt_ref[h * HEAD_DIM:(h + 1) * HEAD_DIM, :], ones_rows], axis=0)
        return k_ref[:, 2 * pair * LANES:(2 * pair + 2) * LANES], rhs, v_ext

    def two_pass_body():
        causal = (j * t + lax.broadcasted_iota(I32, (t, t), 0)
                  <= i * t + lax.broadcasted_iota(I32, (t, t), 1))
        for h in range(N_HEADS):
            lhs, rhs, v_ext = head_operands(h, auxq_ref[N_HEADS + h:N_HEADS + h + 1, :])
            s = jnp.dot(lhs, rhs, preferred_element_type=F32)
            s = jnp.where(causal, s, NEG_INF)
            m_prev = jnp.where(j == 0, M_INIT, m_scr[h:h + 1, :])
            m_new = jnp.maximum(m_prev, _col_reduce(s, jnp.max))
            p = jnp.exp2(s - m_new).astype(BF16)
            m_scr[h:h + 1, :] = m_new
            acc_scr[h] = (acc_scr[h] * jnp.exp2(m_prev - m_new)
                          + jnp.dot(v_ext, p, preferred_element_type=F32))

    def one_pass_body(diag):
        if diag:
            causal = (lax.broadcasted_iota(I32, (t, t), 0)
                      <= lax.broadcasted_iota(I32, (t, t), 1))
        tmax, shifts = [], []
        for pair in range(N_HEADS // 2):
            ops = []
            for h in (2 * pair, 2 * pair + 1):
                cq = auxq_ref[N_HEADS + h:N_HEADS + h + 1, :]
                ck_first = auxk_ref[N_HEADS + h:N_HEADS + h + 1, 0:1]
                ck_last = auxk_ref[N_HEADS + h:N_HEADS + h + 1, t - 1:t]
                shift = jnp.minimum(cq - ck_last, 0.0) - jnp.minimum(cq - ck_first, 0.0)
                shift = jnp.where(j == 0, 0.0, shift)
                shifts.append(shift)
                ops.append(head_operands(h, cq - (m_scr[h:h + 1, :] + shift)))
            x2 = jnp.dot(ops[0][0], jnp.concatenate([ops[0][1], ops[1][1]], axis=1),
                         preferred_element_type=F32)
            for e in range(2):
                x = x2[:, e * t:(e + 1) * t]
                if diag:
                    x = jnp.where(causal, x, NEG_INF)
                tmax.append(_col_reduce(x, jnp.max))
                pv_scr[2 * pair + e] = jnp.dot(ops[e][2], jnp.exp2(x).astype(BF16),
                                               preferred_element_type=F32)
        tmax_all = jnp.concatenate(tmax, axis=0)
        ok = jnp.logical_and(jnp.max(tmax_all) <= MAX_EXP_ARG, jnp.min(tmax_all) >= -MAX_EXP_ARG)

        @pl.when(ok)
        def _():
            for h in range(N_HEADS):
                up = jnp.maximum(tmax[h], 0.0)
                m_scr[h:h + 1, :] = m_scr[h:h + 1, :] + shifts[h] + up
                acc_scr[h] = (acc_scr[h] * jnp.exp2(-shifts[h]) + pv_scr[h]) * jnp.exp2(-up)

        return jnp.where(ok, 1, 0)

    done_scr[0] = 0

    @pl.when(j < i)
    def _():
        done_scr[0] = one_pass_body(False)

    @pl.when(j == i)
    def _():
        done_scr[0] = one_pass_body(True)

    @pl.when(done_scr[0] == 0)
    def _():
        two_pass_body()

    @pl.when(j == i)
    def _():
        for pair in range(N_HEADS // 2):
            halves = [acc_scr[h, 0:HEAD_DIM, :] / acc_scr[h, HEAD_DIM:HEAD_DIM + 1, :]
                      for h in (2 * pair, 2 * pair + 1)]
            o_ref[:, pair * LANES:(pair + 1) * LANES] = (
                jnp.concatenate(halves, axis=0).T.astype(BF16))


def _fox_call(qt, k, vt, auxt, *, t):
    B, S, _ = k.shape
    n = S // t
    neg = np.zeros((N_HEADS, LANES - PIECE_ROWS, t), np.float32)
    for h in range(N_HEADS):
        neg[h, N_PIECES * h:N_PIECES * (h + 1), :] = -1.0
    qi_tbl = np.concatenate([np.full(i + 1, i, np.int32) for i in range(n)])
    kj_tbl = np.concatenate([np.arange(i + 1, dtype=np.int32) for i in range(n)])
    grid_spec = pltpu.PrefetchScalarGridSpec(
        num_scalar_prefetch=2,
        grid=(B, len(qi_tbl)),
        in_specs=[
            pl.BlockSpec((None, W_HEADS, t), lambda b, s, qi, kj: (b, 0, qi[s])),
            pl.BlockSpec((None, t, 2 * W_HEADS), lambda b, s, qi, kj: (b, kj[s], 0)),
            pl.BlockSpec((None, W_HEADS, t), lambda b, s, qi, kj: (b, 0, kj[s])),
            pl.BlockSpec((None, 2 * N_HEADS, t), lambda b, s, qi, kj: (b, 0, qi[s])),
            pl.BlockSpec((None, 2 * N_HEADS, t), lambda b, s, qi, kj: (b, 0, kj[s])),
            pl.BlockSpec((N_HEADS, LANES - PIECE_ROWS, t), lambda b, s, qi, kj: (0, 0, 0)),
        ],
        out_specs=pl.BlockSpec((None, t, W_HEADS), lambda b, s, qi, kj: (b, qi[s], 0)),
        scratch_shapes=[
            pltpu.VMEM((N_HEADS, t), F32),
            pltpu.VMEM((N_HEADS, HEAD_DIM + PIECE_ROWS, t), F32),
            pltpu.VMEM((N_HEADS, HEAD_DIM + PIECE_ROWS, t), F32),
            pltpu.SMEM((1,), I32),
        ],
    )
    return pl.pallas_call(
        functools.partial(_fox_kernel, t=t),
        grid_spec=grid_spec,
        out_shape=jax.ShapeDtypeStruct((B, S, W_HEADS), BF16),
        compiler_params=_params(2),
        name="fox",
    )(jnp.asarray(qi_tbl), jnp.asarray(kj_tbl), qt, k, vt, auxt, auxt, jnp.asarray(neg, BF16))


def _merge_kernel(oa_ref, ob_ref, ga_ref, gb_ref, x_ref, g1_ref, sc_ref, sh_ref, n2_ref,
                  wa_ref, wb_ref, wo_ref, x1_ref, h2_ref):
    ya = jnp.dot(oa_ref[...], wa_ref[...], preferred_element_type=F32)
    yb = jnp.dot(ob_ref[...], wb_ref[...], preferred_element_type=F32)
    mix = ga_ref[...].astype(F32) * ya + gb_ref[...].astype(F32) * yb
    y = jnp.dot(mix.astype(BF16), wo_ref[...], preferred_element_type=F32)
    x1 = x_ref[...] + g1_ref[...] * y
    x1_ref[...] = x1
    h2_ref[...] = _rms_mod(x1, n2_ref[...], sc_ref[...], sh_ref[...]).astype(BF16)


def _merge_call(oa, ob, ga, gb, x, g1, sc2, sh2, n2, wa, wb, wo, *, tm):
    B, S, _ = x.shape
    row = lambda b, s: (b, s, 0)
    const2 = lambda b, s: (0, 0)
    vec = pl.BlockSpec((None, 1, D_MODEL), lambda b, s: (b, 0, 0))
    return pl.pallas_call(
        _merge_kernel,
        grid=(B, S // tm),
        in_specs=[
            pl.BlockSpec((None, tm, W_HEADS), row),
            pl.BlockSpec((None, tm, W_HEADS), row),
            pl.BlockSpec((None, tm, D_MODEL), row),
            pl.BlockSpec((None, tm, D_MODEL), row),
            pl.BlockSpec((None, tm, D_MODEL), row),
            vec, vec, vec,
            pl.BlockSpec((1, D_MODEL), const2),
            pl.BlockSpec((W_HEADS, D_MODEL), const2),
            pl.BlockSpec((W_HEADS, D_MODEL), const2),
            pl.BlockSpec((D_MODEL, D_MODEL), const2),
        ],
        out_specs=[pl.BlockSpec((None, tm, D_MODEL), row),
                   pl.BlockSpec((None, tm, D_MODEL), row)],
        out_shape=[jax.ShapeDtypeStruct((B, S, D_MODEL), F32),
                   jax.ShapeDtypeStruct((B, S, D_MODEL), BF16)],
        compiler_params=_params(2),
        name="merge",
    )(oa, ob, ga, gb, x, g1, sc2, sh2, n2, wa, wb, wo)


def _ffn_kernel(h_ref, x_ref, g2_ref, wup_ref, cw_ref, cb_ref, wdn_ref, fg_ref, o_ref,
                carry_scr, *, tm, fc, final):
    s_idx = pl.program_id(1)

    @pl.when(s_idx == 0)
    def _():
        carry_scr[...] = jnp.zeros_like(carry_scr)

    hb = h_ref[...]
    row = lax.broadcasted_iota(I32, (SUBLANES, fc), 0)
    acc = jnp.zeros((tm, D_MODEL), F32)
    for cblk in range(D_FF // fc):
        halves = []
        for part in range(2):
            col = part * D_FF + cblk * fc
            u = jnp.dot(hb, wup_ref[:, col:col + fc], preferred_element_type=F32)
            prev = carry_scr[:, col:col + fc]
            r1 = pltpu.roll(u, 1, 0)
            r2 = pltpu.roll(u, 2, 0)
            head1 = jnp.where(row == 0, prev[7:8], r1[:SUBLANES])
            head2 = jnp.where(row == 0, prev[6:7], jnp.where(row == 1, prev[7:8], r2[:SUBLANES]))
            u1 = jnp.concatenate([head1, r1[SUBLANES:]], axis=0)
            u2 = jnp.concatenate([head2, r2[SUBLANES:]], axis=0)
            carry_scr[:, col:col + fc] = u[tm - SUBLANES:tm]
            cw = cw_ref[:, col:col + fc]
            halves.append(cb_ref[:, col:col + fc] + (cw[0:1] * u2 + cw[1:2] * u1 + cw[2:3] * u))
        act = (jax.nn.silu(halves[0]) * halves[1]).astype(BF16)
        acc = acc + jnp.dot(act, wdn_ref[cblk * fc:(cblk + 1) * fc, :],
                            preferred_element_type=F32)
    x2 = x_ref[...] + g2_ref[...] * acc
    if final:
        var = jnp.mean(x2 * x2, axis=-1, keepdims=True)
        x2 = x2 * lax.rsqrt(var + EPS) * fg_ref[...]
    o_ref[...] = x2


def _ffn_call(h2, x1, g2, wup, cw, cb, wdn, fg, *, tm, fc, final):
    B, S, _ = x1.shape
    row = lambda b, s: (b, s, 0)
    const2 = lambda b, s: (0, 0)
    return pl.pallas_call(
        functools.partial(_ffn_kernel, tm=tm, fc=fc, final=final),
        grid=(B, S // tm),
        in_specs=[
            pl.BlockSpec((None, tm, D_MODEL), row),
            pl.BlockSpec((None, tm, D_MODEL), row),
            pl.BlockSpec((None, 1, D_MODEL), lambda b, s: (b, 0, 0)),
            pl.BlockSpec((D_MODEL, 2 * D_FF), const2, pipeline_mode=pl.Buffered(1)),
            pl.BlockSpec((SUBLANES, 2 * D_FF), const2),
            pl.BlockSpec((1, 2 * D_FF), const2),
            pl.BlockSpec((D_FF, D_MODEL), const2, pipeline_mode=pl.Buffered(1)),
            pl.BlockSpec((1, D_MODEL), const2),
        ],
        out_specs=pl.BlockSpec((None, tm, D_MODEL), row),
        out_shape=jax.ShapeDtypeStruct((B, S, D_MODEL), F32),
        scratch_shapes=[pltpu.VMEM((SUBLANES, 2 * D_FF), F32)],
        compiler_params=_params(2),
        name="ffn",
    )(h2, x1, g2, wup, cw, cb, wdn, fg)


def _reorder_w_in(w):
    o = np.cumsum([0, W_HEADS, HEAD_DIM, HEAD_DIM, W_HEADS, HEAD_DIM, N_HEADS,
                   W_HEADS, W_HEADS, W_HEADS, N_HEADS, 2 * D_MODEL])
    seg = lambda k: w[:, o[k]:o[k + 1]]
    q_a, k_a, v_a, q_i, k_i, w_i, q_f, k_f, v_f, f_f, gates = [seg(k) for k in range(11)]
    pad = jnp.zeros((w.shape[0], N_COLS - int(o[-1])), w.dtype)
    return jnp.concatenate([q_a, q_i, q_f, k_f, v_f, gates, k_i, k_a, v_a, w_i, f_f, pad],
                           axis=1).astype(BF16)


def kernel(x, c, positions, mod_w, mod_b, norm1_g, norm2_g, w_in, forget_bias, w_branch_a,
           w_branch_b, w_out, w_up, conv_w, conv_b, w_down, final_g):
    B, S, _ = x.shape
    depth = mod_w.shape[0]
    tm = min(TOKEN_TILE, S)
    mod = _mod_call(c, mod_w, mod_b)[:, :, :B].reshape(depth, 6, B, 1, D_MODEL)
    pos = positions.astype(F32).reshape(B, S, 1)

    inv_freq = ROPE_THETA ** (-jnp.arange(0, ROT_DIM, 2, dtype=F32) / ROT_DIM)
    l64 = np.arange(LANES) % HEAD_DIM
    invf = jnp.where(l64 < ROT_DIM, inv_freq[l64 % (ROT_DIM // 2)], 0.0).reshape(1, LANES)

    place = np.zeros((N_PIECES, LANES, LANES), np.float32)
    for h in range(N_HEADS):
        for piece in range(N_PIECES):
            place[piece, LANE_FORGET + h, LANE_CK + N_PIECES * h + piece] = 1.0
    place = jnp.asarray(place, BF16)

    for l in range(depth):
        sh1, sc1, g1, sh2, sc2, g2 = [mod[l, k] for k in range(6)]
        fb = jnp.zeros((1, LANES), F32).at[0, LANE_FORGET:LANE_FORGET + N_HEADS].set(forget_bias[l])
        (qat, qit, qft, kf, vft, ga, gb, kk, vat, auxt) = _in_call(
            x, sh1, sc1, norm1_g[l].reshape(1, D_MODEL), pos, invf, fb, place,
            _reorder_w_in(w_in[l]), tm=tm)
        oa = _dsa_call(qit, qat, auxt, kk, vat, q_blk=DSA_Q_BLOCK, tk=min(DSA_KEY_TILE, S))
        ob = _fox_call(qft, kf, vft, auxt, t=tm)
        x1, h2 = _merge_call(oa, ob, ga, gb, x, g1, sc2, sh2, norm2_g[l].reshape(1, D_MODEL),
                             w_branch_a[l].astype(BF16), w_branch_b[l].astype(BF16),
                             w_out[l].astype(BF16), tm=tm)
        cw8 = jnp.zeros((SUBLANES, 2 * D_FF), F32).at[:conv_w.shape[1]].set(conv_w[l])
        x = _ffn_call(h2, x1, g2, w_up[l].astype(BF16), cw8, conv_b[l].reshape(1, 2 * D_FF),
                      w_down[l].astype(BF16), final_g.reshape(1, D_MODEL),
                      tm=tm, fc=FFN_CHUNK, final=(l == depth - 1))
    return x
```

```python
import functools

import numpy as np
import jax
import jax.numpy as jnp
from jax import lax
from jax.experimental import pallas as pl
from jax.experimental.pallas import tpu as pltpu

F32 = jnp.float32
BF16 = jnp.bfloat16
I32 = jnp.int32

D_MODEL = 1024
CHUNK = 64
HEAD_DIM = 64
N_HEADS = 8
TOPK_MAX = 256
ROPE_THETA = 500000.0
ROT_DIM = HEAD_DIM // 4
D_FF = 2816
EPS = 1e-6
NEG_INF = -1e30
W_HEADS = N_HEADS * HEAD_DIM

LANES = 128
SUBLANES = 8
VMEM_LIMIT = 56 * 1024 * 1024

TOKEN_TILE = 512
FOX_TILE = 1024
DSA_Q_BLOCK = 2 * LANES
DSA_KEY_TILE = 512
COUNT_ROWS = 32
FFN_CHUNK = 2816

COL_QA = 0
COL_QI = 512
COL_QF = 1024
COL_KF = 1536
COL_VF = 2048
COL_GATES = 2560
COL_SMALL = 4608
N_COLS = 4864
LANE_WIDX = 64
LANE_FORGET = 72
LANE_ONES = 80
N_PIECES = 3
PIECE_ROWS = 16
LANE_CK = PIECE_ROWS

INT_MIN = -(2 ** 31)
M_INIT = -3.0e38
LOG2E = 1.4426950408889634
MAX_EXP_ARG = 100.0


def _f32_key(v):
    b = int(np.float32(v).view(np.int32))
    return b ^ ((b >> 31) & 0x7FFFFFFF)


MIN_VALID_KEY = _f32_key(0.5 * NEG_INF) + 1


def _params(n_grid):
    return pltpu.CompilerParams(
        dimension_semantics=("arbitrary",) * n_grid, vmem_limit_bytes=VMEM_LIMIT)


def _col_reduce(x, op):
    rows = x.shape[0]
    part = op(x.reshape(rows // SUBLANES, SUBLANES, x.shape[1]), axis=0)
    return op(part, axis=0, keepdims=True)


def _mod_kernel(c_ref, w_ref, b_ref, o_ref):
    o_ref[...] = jnp.dot(c_ref[...], w_ref[...], precision=lax.Precision.HIGHEST,
                         preferred_element_type=F32) + b_ref[...]


def _mod_call(c, mod_w, mod_b):
    depth = mod_w.shape[0]
    c8 = jnp.zeros((SUBLANES, D_MODEL), F32).at[: c.shape[0]].set(c)
    b4 = mod_b.reshape(depth, 6, 1, D_MODEL)
    return pl.pallas_call(
        _mod_kernel,
        grid=(depth, 6),
        in_specs=[
            pl.BlockSpec((SUBLANES, D_MODEL), lambda l, j: (0, 0)),
            pl.BlockSpec((None, D_MODEL, D_MODEL), lambda l, j: (l, 0, j)),
            pl.BlockSpec((None, None, 1, D_MODEL), lambda l, j: (l, j, 0, 0)),
        ],
        out_specs=pl.BlockSpec((None, None, SUBLANES, D_MODEL), lambda l, j: (l, j, 0, 0)),
        out_shape=jax.ShapeDtypeStruct((depth, 6, SUBLANES, D_MODEL), F32),
        compiler_params=_params(2),
        name="mod",
    )(c8, mod_w, b4)


def _rms_mod(x, gain, scale, shift):
    var = jnp.mean(x * x, axis=-1, keepdims=True)
    y = x * lax.rsqrt(var + EPS) * gain
    return y * (1.0 + scale) + shift


def _split3(x):
    p0 = x.astype(BF16)
    r1 = x - p0.astype(F32)
    p1 = r1.astype(BF16)
    return p0, p1, (r1 - p1.astype(F32)).astype(BF16)


def _in_kernel(x_ref, sh_ref, sc_ref, g_ref, pos_ref, invf_ref, fb_ref, place_ref, w_ref,
               qat_ref, qit_ref, qft_ref, kf_ref, vft_ref, ga_ref, gb_ref,
               kk_ref, vat_ref, auxt_ref, carry_scr, *, tm):
    s_idx = pl.program_id(1)

    @pl.when(s_idx == 0)
    def _():
        carry_scr[...] = jnp.zeros_like(carry_scr)

    h = _rms_mod(x_ref[...], g_ref[...], sc_ref[...], sh_ref[...])
    hb = h.astype(BF16)

    ang = pos_ref[...] * invf_ref[...]
    cos = jnp.cos(ang)
    sin = jnp.sin(ang)
    l64 = lax.broadcasted_iota(I32, (tm, LANES), 1) & (HEAD_DIM - 1)
    half = ROT_DIM // 2
    t_cos = jnp.where(l64 < ROT_DIM, cos, 1.0)
    t_lo = jnp.where(l64 < half, -sin, 0.0)
    t_hi = jnp.where(l64 < half, 0.0, jnp.where(l64 < ROT_DIM, sin, 0.0))

    def rope(xc):
        return (xc * t_cos + pltpu.roll(xc, LANES - half, 1) * t_lo
                + pltpu.roll(xc, half, 1) * t_hi)

    def proj(col, width):
        return jnp.dot(hb, w_ref[:, col:col + width], preferred_element_type=F32)

    for col, out_ref, scale in ((COL_QA, qat_ref, 0.125 * LOG2E), (COL_QI, qit_ref, 0.125)):
        p = proj(col, W_HEADS)
        for cblk in range(W_HEADS // LANES):
            r = (rope(p[:, cblk * LANES:(cblk + 1) * LANES]) * scale).T.astype(BF16)
            out_ref[2 * cblk] = r[:HEAD_DIM]
            out_ref[2 * cblk + 1] = r[HEAD_DIM:]

    for col, out_ref, scale in ((COL_QF, qft_ref, 0.125 * LOG2E), (COL_VF, vft_ref, 1.0)):
        p = proj(col, W_HEADS)
        for cblk in range(W_HEADS // LANES):
            sl = slice(cblk * LANES, (cblk + 1) * LANES)
            out_ref[sl, :] = (p[:, sl] * scale).T.astype(BF16)
    ga_ref[...] = jax.nn.sigmoid(proj(COL_GATES, D_MODEL)).astype(BF16)
    gb_ref[...] = jax.nn.sigmoid(proj(COL_GATES + D_MODEL, D_MODEL)).astype(BF16)

    lane = lax.broadcasted_iota(I32, (tm, LANES), 1)
    ones_lanes = jnp.where(lane < N_PIECES, 1.0, 0.0)
    small = proj(COL_SMALL, 2 * LANES)
    kk_ref[...] = jnp.concatenate([rope(small[:, :LANES]), ones_lanes], axis=1).astype(BF16)
    blk = small[:, LANES:]

    is_f = (lane >= LANE_FORGET) & (lane < LANE_FORGET + N_HEADS)
    z = blk + fb_ref[...]
    logf = jnp.minimum(z, 0.0) - jnp.log1p(jnp.exp(-jnp.abs(z)))
    logf = jnp.where(is_f, logf, 0.0)
    tri = (lax.broadcasted_iota(I32, (tm, tm), 0)
           >= lax.broadcasted_iota(I32, (tm, tm), 1)).astype(BF16)
    p0, p1, p2 = _split3(logf)
    cum = (jnp.dot(tri, p0, preferred_element_type=F32)
           + jnp.dot(tri, p1, preferred_element_type=F32)
           + jnp.dot(tri, p2, preferred_element_type=F32)) + carry_scr[0:1, :]
    carry_scr[0:1, :] = cum[tm - 1:tm, :]
    cum = cum * LOG2E

    c0, c1, c2 = _split3(cum)
    kaux = (jnp.dot(c0, place_ref[0], preferred_element_type=F32)
            + jnp.dot(c1, place_ref[1], preferred_element_type=F32)
            + jnp.dot(c2, place_ref[2], preferred_element_type=F32) + ones_lanes).astype(BF16)
    kf = proj(COL_KF, W_HEADS).astype(BF16)
    for pair in range(N_HEADS // 2):
        kf_ref[:, 2 * pair * LANES:(2 * pair + 1) * LANES] = kf[:, pair * LANES:(pair + 1) * LANES]
        kf_ref[:, (2 * pair + 1) * LANES:(2 * pair + 2) * LANES] = kaux

    is_w = (lane >= LANE_WIDX) & (lane < LANE_WIDX + N_HEADS)
    comb = jnp.where(is_w, blk * (N_HEADS ** -0.5), jnp.where(is_f, cum, blk))
    comb_t = jnp.where(lane == LANE_ONES, 1.0, comb).T
    vat_ref[...] = comb_t.astype(BF16)
    auxt_ref[...] = comb_t[LANE_WIDX:LANE_WIDX + 2 * N_HEADS]


def _in_call(x, sh, sc, gain, pos, invf, fb, place, w_all, *, tm):
    B, S, _ = x.shape
    hm_t = jax.ShapeDtypeStruct((B, N_HEADS, HEAD_DIM, S), BF16)
    wide_t = jax.ShapeDtypeStruct((B, W_HEADS, S), BF16)
    gate = jax.ShapeDtypeStruct((B, S, D_MODEL), BF16)
    row = lambda b, s: (b, s, 0)
    col = lambda b, s: (b, 0, s)
    vec = pl.BlockSpec((None, 1, D_MODEL), lambda b, s: (b, 0, 0))
    const2 = lambda b, s: (0, 0)
    hm_spec = pl.BlockSpec((None, N_HEADS, HEAD_DIM, tm), lambda b, s: (b, 0, 0, s))
    return pl.pallas_call(
        functools.partial(_in_kernel, tm=tm),
        grid=(B, S // tm),
        in_specs=[
            pl.BlockSpec((None, tm, D_MODEL), row),
            vec, vec,
            pl.BlockSpec((1, D_MODEL), const2),
            pl.BlockSpec((None, tm, 1), row),
            pl.BlockSpec((1, LANES), const2),
            pl.BlockSpec((1, LANES), const2),
            pl.BlockSpec((N_PIECES, LANES, LANES), lambda b, s: (0, 0, 0)),
            pl.BlockSpec((D_MODEL, N_COLS), const2, pipeline_mode=pl.Buffered(1)),
        ],
        out_specs=[
            hm_spec, hm_spec,
            pl.BlockSpec((None, W_HEADS, tm), col),
            pl.BlockSpec((None, tm, 2 * W_HEADS), row),
            pl.BlockSpec((None, W_HEADS, tm), col),
            pl.BlockSpec((None, tm, D_MODEL), row),
            pl.BlockSpec((None, tm, D_MODEL), row),
            pl.BlockSpec((None, tm, 2 * LANES), row),
            pl.BlockSpec((None, LANES, tm), col),
            pl.BlockSpec((None, 2 * N_HEADS, tm), col),
        ],
        out_shape=[hm_t, hm_t, wide_t,
                   jax.ShapeDtypeStruct((B, S, 2 * W_HEADS), BF16),
                   wide_t, gate, gate,
                   jax.ShapeDtypeStruct((B, S, 2 * LANES), BF16),
                   jax.ShapeDtypeStruct((B, LANES, S), BF16),
                   jax.ShapeDtypeStruct((B, 2 * N_HEADS, S), F32)],
        scratch_shapes=[pltpu.VMEM((SUBLANES, LANES), F32)],
        compiler_params=_params(2),
        name="in_proj",
    )(x, sh, sc, gain, pos, invf, fb, place, w_all)


def _dsa_kernel(qit_ref, qat_ref, aux_ref, kk_ref, vat_ref, o_ref,
                keys_scr, acc_scr, p_scr, rhs_scr, *, q_blk, tk, topk, idx_bits):
    i = pl.program_id(1)
    width = N_HEADS * q_blk
    n_tiles = ((i + 1) * q_blk + tk - 1) // tk

    zeros = jnp.zeros((HEAD_DIM, width), BF16)
    qi_ext = jnp.concatenate(
        [jnp.concatenate([qit_ref[h] for h in range(N_HEADS)], axis=1), zeros], axis=0)
    rhs_scr[...] = jnp.concatenate(
        [zeros, jnp.concatenate([qat_ref[h] for h in range(N_HEADS)], axis=1),
         jnp.zeros((LANES, width), BF16)], axis=0)
    piece_row = lax.broadcasted_iota(I32, (PIECE_ROWS, width), 0)

    def set_row_term(r):
        hi = r.astype(BF16).astype(F32)
        mid = (r - hi).astype(BF16).astype(F32)
        lo = r - hi - mid
        blk = jnp.where(piece_row == 0, hi, jnp.where(piece_row == 1, mid,
                                                      jnp.where(piece_row == 2, lo, 0.0)))
        rhs_scr[2 * HEAD_DIM:2 * HEAD_DIM + PIECE_ROWS, :] = blk.astype(BF16)
    w_rows = aux_ref[0:N_HEADS, :]
    q_pos = i * q_blk + lax.broadcasted_iota(I32, (1, q_blk), 1)
    key_lim = (q_pos // CHUNK + 1) * CHUNK
    row_iota = lax.broadcasted_iota(I32, (tk, q_blk), 0)

    def score_tile(t, _):
        off = pl.multiple_of(t * tk, tk)
        s = jnp.dot(kk_ref[pl.ds(off, tk), 0:LANES], qi_ext,
                    preferred_element_type=F32)
        acc = jnp.zeros((tk, q_blk), F32)
        for h in range(N_HEADS):
            acc = acc + w_rows[h:h + 1, :] * jnp.maximum(s[:, h * q_blk:(h + 1) * q_blk], 0.0)
        bits = pltpu.bitcast(acc, I32)
        key = bits ^ ((bits >> 31) & 0x7FFFFFFF)
        keys_scr[pl.ds(off, tk), :] = jnp.where(off + row_iota < key_lim, key, INT_MIN)
        return 0

    lax.fori_loop(0, n_tiles, score_tile, 0)

    count_iota = lax.broadcasted_iota(I32, (COUNT_ROWS, q_blk), 0)

    def count(pred):
        def tile(t, acc):
            off = pl.multiple_of(t * tk, tk)
            for r in range(tk // COUNT_ROWS):
                start = off + r * COUNT_ROWS
                acc = acc + pred(keys_scr[pl.ds(start, COUNT_ROWS), :], start + count_iota)
            return acc
        acc = lax.fori_loop(0, n_tiles, tile, jnp.zeros((COUNT_ROWS, q_blk), I32))
        return _col_reduce(acc, jnp.sum)

    def radix_pass(b, carry):
        prefix, cnt_ge = carry
        cand = prefix | jnp.left_shift(jnp.int32(1), 31 - b)
        cand_s = cand ^ INT_MIN
        cnt = count(lambda k, _: jnp.where(k >= cand_s, 1, 0))
        ok = cnt >= topk
        return jnp.where(ok, cand, prefix), jnp.where(ok, cnt, cnt_ge)

    total = jnp.zeros((1, q_blk), I32) + n_tiles * tk
    prefix, cnt_ge = lax.fori_loop(0, 32, radix_pass, (jnp.zeros((1, q_blk), I32), total))
    kth = prefix ^ INT_MIN
    thr = jnp.maximum(kth, MIN_VALID_KEY)

    has_tie = jnp.where(cnt_ge > topk, jnp.where(kth >= MIN_VALID_KEY, 1, 0), 0)

    @pl.when(jnp.max(has_tie) > 0)
    def _():
        n_gt = count(lambda k, _: jnp.where(k > thr, 1, 0))
        need = topk - n_gt

        def idx_pass(b, pre):
            cand = pre | jnp.left_shift(jnp.int32(1), idx_bits - 1 - b)
            c = count(lambda k, kidx: jnp.where(k == thr, jnp.where(kidx < cand, 1, 0), 0))
            return jnp.where(c < need, cand, pre)

        last = lax.fori_loop(0, idx_bits, idx_pass, jnp.zeros((1, q_blk), I32))

        def drop_tile(t, _):
            off = pl.multiple_of(t * tk, tk)
            k = keys_scr[pl.ds(off, tk), :]
            drop = jnp.where(k == thr, jnp.where(off + row_iota > last, INT_MIN, k), k)
            keys_scr[pl.ds(off, tk), :] = drop
            return 0

        lax.fori_loop(0, n_tiles, drop_tile, 0)

    acc_scr[...] = jnp.zeros_like(acc_scr)

    heads = [slice(h * q_blk, (h + 1) * q_blk) for h in range(N_HEADS)]

    def lane_row(x8):
        return jnp.concatenate([x8[h:h + 1, :] for h in range(N_HEADS)], axis=1)

    def tile_operands(t):
        off = pl.multiple_of(t * tk, tk)
        bias = jnp.where(keys_scr[pl.ds(off, tk), :] >= thr, 0.0, NEG_INF)
        return kk_ref[pl.ds(off, tk), :], bias, vat_ref[:, pl.ds(off, tk)]

    def attn_tile(t, m_all):
        kt, bias, vt = tile_operands(t)
        set_row_term(jnp.zeros((1, width), F32))
        logits = jnp.dot(kt, rhs_scr[...], preferred_element_type=F32)
        m_rows = []
        for h in range(N_HEADS):
            s = logits[:, heads[h]] + bias
            m_new = jnp.maximum(m_all[h:h + 1, :], _col_reduce(s, jnp.max))
            p_scr[:, heads[h]] = jnp.exp2(s - m_new).astype(BF16)
            m_rows.append(m_new)
        m_new = jnp.concatenate(m_rows, axis=0)
        pv = jnp.dot(vt, p_scr[...], preferred_element_type=F32)
        acc_scr[...] = acc_scr[...] * lane_row(jnp.exp2(m_all - m_new)) + pv
        return m_new

    def one_pass_tile(t, carry):
        m_ref, worst = carry
        kt, bias, vt = tile_operands(t)
        set_row_term(-lane_row(m_ref))
        x_all = jnp.dot(kt, rhs_scr[...], preferred_element_type=F32)
        tmax = []
        for h in range(N_HEADS):
            x = x_all[:, heads[h]] + bias
            p_scr[:, heads[h]] = jnp.exp2(x).astype(BF16)
            tmax.append(_col_reduce(x, jnp.max))
        tmax = jnp.concatenate(tmax, axis=0)
        up = jnp.maximum(tmax, 0.0)
        pv = jnp.dot(vt, p_scr[...], preferred_element_type=F32)
        acc_scr[...] = (acc_scr[...] + pv) * lane_row(jnp.exp2(-up))
        return m_ref + up, jnp.maximum(worst, tmax)

    m0 = jnp.full((N_HEADS, q_blk), M_INIT, F32)
    m1 = attn_tile(0, m0)
    _, worst = lax.fori_loop(1, n_tiles, one_pass_tile, (m1, m0))

    @pl.when(jnp.max(worst) > MAX_EXP_ARG)
    def _():
        acc_scr[...] = jnp.zeros_like(acc_scr)
        lax.fori_loop(0, n_tiles, attn_tile, m0)

    denom = acc_scr[LANE_ONES:LANE_ONES + 1, :]
    for pair in range(N_HEADS // 2):
        halves = []
        for h in (2 * pair, 2 * pair + 1):
            halves.append(acc_scr[0:HEAD_DIM, heads[h]] / denom[:, heads[h]])
        o_ref[:, pair * LANES:(pair + 1) * LANES] = (
            jnp.concatenate(halves, axis=0).T.astype(BF16))


def _dsa_call(qit, qat, auxt, kk, vat, *, q_blk, tk):
    B, _, _, S = qit.shape
    topk = min(TOPK_MAX, S // 4)
    idx_bits = max(1, int(np.ceil(np.log2(S))))
    width = N_HEADS * q_blk
    hm_spec = pl.BlockSpec((None, N_HEADS, HEAD_DIM, q_blk), lambda b, i: (b, 0, 0, i))
    return pl.pallas_call(
        functools.partial(_dsa_kernel, q_blk=q_blk, tk=tk, topk=topk, idx_bits=idx_bits),
        grid=(B, S // q_blk),
        in_specs=[
            hm_spec, hm_spec,
            pl.BlockSpec((None, 2 * N_HEADS, q_blk), lambda b, i: (b, 0, i)),
            pl.BlockSpec((None, S, 2 * LANES), lambda b, i: (b, 0, 0)),
            pl.BlockSpec((None, LANES, S), lambda b, i: (b, 0, 0)),
        ],
        out_specs=pl.BlockSpec((None, q_blk, W_HEADS), lambda b, i: (b, i, 0)),
        out_shape=jax.ShapeDtypeStruct((B, S, W_HEADS), BF16),
        scratch_shapes=[
            pltpu.VMEM((S, q_blk), I32),
            pltpu.VMEM((LANES, width), F32),
            pltpu.VMEM((tk, width), BF16),
            pltpu.VMEM((2 * LANES, width), BF16),
        ],
        compiler_params=_params(2),
        name="dsa",
    )(qit, qat, auxt, kk, vat)


def _fox_kernel(qi_tbl, kj_tbl, qt_ref, k_ref, vt_ref, auxq_ref, auxk_ref, neg_ref, o_ref,
                m_scr, acc_scr, pv_scr, done_scr, *, t):
    step = pl.program_id(1)
    i = qi_tbl[step]
    j = kj_tbl[step]
    top = lax.broadcasted_iota(I32, (LANES, t), 0) < HEAD_DIM
    piece_row = lax.broadcasted_iota(I32, (PIECE_ROWS, t), 0)
    ones_rows = jnp.ones((PIECE_ROWS, t), BF16)

    @pl.when(j == 0)
    def _():
        m_scr[...] = jnp.minimum(auxq_ref[N_HEADS:2 * N_HEADS, :]
                                 - auxk_ref[N_HEADS:2 * N_HEADS, t - 1:t], 0.0)
        acc_scr[...] = jnp.zeros_like(acc_scr)

    def head_operands(h, r):
        pair, e = divmod(h, 2)
        qp = qt_ref[pair * LANES:(pair + 1) * LANES, :]
        qm = jnp.where(top, qp, 0) if e == 0 else jnp.where(top, 0, qp)
        hi = r.astype(BF16).astype(F32)
        mid = (r - hi).astype(BF16).astype(F32)
        lo = r - hi - mid
        pieces = jnp.where(piece_row == 0, hi, jnp.where(piece_row == 1, mid,
                                                         jnp.where(piece_row == 2, lo, 0.0)))
        rhs = jnp.concatenate([qm, pieces.astype(BF16), neg_ref[h]], axis=0)
        v_ext = jnp.concatenate([vt_ref[h * HEAD_DIM:(h + 1) * HEAD_DIM, :], ones_rows], axis=0)
        return k_ref[:, 2 * pair * LANES:(2 * pair + 2) * LANES], rhs, v_ext

    def two_pass_body():
        causal = (j * t + lax.broadcasted_iota(I32, (t, t), 0)
                  <= i * t + lax.broadcasted_iota(I32, (t, t), 1))
        for h in range(N_HEADS):
            lhs, rhs, v_ext = head_operands(h, auxq_ref[N_HEADS + h:N_HEADS + h + 1, :])
            s = jnp.dot(lhs, rhs, preferred_element_type=F32)
            s = jnp.where(causal, s, NEG_INF)
            m_prev = jnp.where(j == 0, M_INIT, m_scr[h:h + 1, :])
            m_new = jnp.maximum(m_prev, _col_reduce(s, jnp.max))
            p = jnp.exp2(s - m_new).astype(BF16)
            m_scr[h:h + 1, :] = m_new
            acc_scr[h] = (acc_scr[h] * jnp.exp2(m_prev - m_new)
                          + jnp.dot(v_ext, p, preferred_element_type=F32))

    def one_pass_body(diag):
        if diag:
            causal = (lax.broadcasted_iota(I32, (t, t), 0)
                      <= lax.broadcasted_iota(I32, (t, t), 1))
        tmax, shifts = [], []
        for pair in range(N_HEADS // 2):
            ops = []
            for h in (2 * pair, 2 * pair + 1):
                cq = auxq_ref[N_HEADS + h:N_HEADS + h + 1, :]
                ck_first = auxk_ref[N_HEADS + h:N_HEADS + h + 1, 0:1]
                ck_last = auxk_ref[N_HEADS + h:N_HEADS + h + 1, t - 1:t]
                shift = jnp.minimum(cq - ck_last, 0.0) - jnp.minimum(cq - ck_first, 0.0)
                shift = jnp.where(j == 0, 0.0, shift)
                shifts.append(shift)
                ops.append(head_operands(h, cq - (m_scr[h:h + 1, :] + shift)))
            x2 = jnp.dot(ops[0][0], jnp.concatenate([ops[0][1], ops[1][1]], axis=1),
                         preferred_element_type=F32)
            for e in range(2):
                x = x2[:, e * t:(e + 1) * t]
                if diag:
                    x = jnp.where(causal, x, NEG_INF)
                tmax.append(_col_reduce(x, jnp.max))
                pv_scr[2 * pair + e] = jnp.dot(ops[e][2], jnp.exp2(x).astype(BF16),
                                               preferred_element_type=F32)
        tmax_all = jnp.concatenate(tmax, axis=0)
        ok = jnp.logical_and(jnp.max(tmax_all) <= MAX_EXP_ARG, jnp.min(tmax_all) >= -MAX_EXP_ARG)

        @pl.when(ok)
        def _():
            for h in range(N_HEADS):
                up = jnp.maximum(tmax[h], 0.0)
                m_scr[h:h + 1, :] = m_scr[h:h + 1, :] + shifts[h] + up
                acc_scr[h] = (acc_scr[h] * jnp.exp2(-shifts[h]) + pv_scr[h]) * jnp.exp2(-up)

        return jnp.where(ok, 1, 0)

    done_scr[0] = 0

    @pl.when(j < i)
    def _():
        done_scr[0] = one_pass_body(False)

    @pl.when(j == i)
    def _():
        done_scr[0] = one_pass_body(True)

    @pl.when(done_scr[0] == 0)
    def _():
        two_pass_body()

    @pl.when(j == i)
    def _():
        for pair in range(N_HEADS // 2):
            halves = [acc_scr[h, 0:HEAD_DIM, :] / acc_scr[h, HEAD_DIM:HEAD_DIM + 1, :]
                      for h in (2 * pair, 2 * pair + 1)]
            o_ref[:, pair * LANES:(pair + 1) * LANES] = (
                jnp.concatenate(halves, axis=0).T.astype(BF16))


def _fox_call(qt, k, vt, auxt, *, t):
    B, S, _ = k.shape
    n = S // t
    neg = np.zeros((N_HEADS, LANES - PIECE_ROWS, t), np.float32)
    for h in range(N_HEADS):
        neg[h, N_PIECES * h:N_PIECES * (h + 1), :] = -1.0
    qi_tbl = np.concatenate([np.full(i + 1, i, np.int32) for i in range(n)])
    kj_tbl = np.concatenate([np.arange(i + 1, dtype=np.int32) for i in range(n)])
    grid_spec = pltpu.PrefetchScalarGridSpec(
        num_scalar_prefetch=2,
        grid=(B, len(qi_tbl)),
        in_specs=[
            pl.BlockSpec((None, W_HEADS, t), lambda b, s, qi, kj: (b, 0, qi[s])),
            pl.BlockSpec((None, t, 2 * W_HEADS), lambda b, s, qi, kj: (b, kj[s], 0)),
            pl.BlockSpec((None, W_HEADS, t), lambda b, s, qi, kj: (b, 0, kj[s])),
            pl.BlockSpec((None, 2 * N_HEADS, t), lambda b, s, qi, kj: (b, 0, qi[s])),
            pl.BlockSpec((None, 2 * N_HEADS, t), lambda b, s, qi, kj: (b, 0, kj[s])),
            pl.BlockSpec((N_HEADS, LANES - PIECE_ROWS, t), lambda b, s, qi, kj: (0, 0, 0)),
        ],
        out_specs=pl.BlockSpec((None, t, W_HEADS), lambda b, s, qi, kj: (b, qi[s], 0)),
        scratch_shapes=[
            pltpu.VMEM((N_HEADS, t), F32),
            pltpu.VMEM((N_HEADS, HEAD_DIM + PIECE_ROWS, t), F32),
            pltpu.VMEM((N_HEADS, HEAD_DIM + PIECE_ROWS, t), F32),
            pltpu.SMEM((1,), I32),
        ],
    )
    return pl.pallas_call(
        functools.partial(_fox_kernel, t=t),
        grid_spec=grid_spec,
        out_shape=jax.ShapeDtypeStruct((B, S, W_HEADS), BF16),
        compiler_params=_params(2),
        name="fox",
    )(jnp.asarray(qi_tbl), jnp.asarray(kj_tbl), qt, k, vt, auxt, auxt, jnp.asarray(neg, BF16))


def _merge_kernel(oa_ref, ob_ref, ga_ref, gb_ref, x_ref, g1_ref, sc_ref, sh_ref, n2_ref,
                  wa_ref, wb_ref, wo_ref, x1_ref, h2_ref):
    ya = jnp.dot(oa_ref[...], wa_ref[...], preferred_element_type=F32)
    yb = jnp.dot(ob_ref[...], wb_ref[...], preferred_element_type=F32)
    mix = ga_ref[...].astype(F32) * ya + gb_ref[...].astype(F32) * yb
    y = jnp.dot(mix.astype(BF16), wo_ref[...], preferred_element_type=F32)
    x1 = x_ref[...] + g1_ref[...] * y
    x1_ref[...] = x1
    h2_ref[...] = _rms_mod(x1, n2_ref[...], sc_ref[...], sh_ref[...]).astype(BF16)


def _merge_call(oa, ob, ga, gb, x, g1, sc2, sh2, n2, wa, wb, wo, *, tm):
    B, S, _ = x.shape
    row = lambda b, s: (b, s, 0)
    const2 = lambda b, s: (0, 0)
    vec = pl.BlockSpec((None, 1, D_MODEL), lambda b, s: (b, 0, 0))
    return pl.pallas_call(
        _merge_kernel,
        grid=(B, S // tm),
        in_specs=[
            pl.BlockSpec((None, tm, W_HEADS), row),
            pl.BlockSpec((None, tm, W_HEADS), row),
            pl.BlockSpec((None, tm, D_MODEL), row),
            pl.BlockSpec((None, tm, D_MODEL), row),
            pl.BlockSpec((None, tm, D_MODEL), row),
            vec, vec, vec,
            pl.BlockSpec((1, D_MODEL), const2),
            pl.BlockSpec((W_HEADS, D_MODEL), const2),
            pl.BlockSpec((W_HEADS, D_MODEL), const2),
            pl.BlockSpec((D_MODEL, D_MODEL), const2),
        ],
        out_specs=[pl.BlockSpec((None, tm, D_MODEL), row),
                   pl.BlockSpec((None, tm, D_MODEL), row)],
        out_shape=[jax.ShapeDtypeStruct((B, S, D_MODEL), F32),
                   jax.ShapeDtypeStruct((B, S, D_MODEL), BF16)],
        compiler_params=_params(2),
        name="merge",
    )(oa, ob, ga, gb, x, g1, sc2, sh2, n2, wa, wb, wo)


def _ffn_kernel(h_ref, x_ref, g2_ref, wup_ref, cw_ref, cb_ref, wdn_ref, fg_ref, o_ref,
                carry_scr, *, tm, fc, final):
    s_idx = pl.program_id(1)

    @pl.when(s_idx == 0)
    def _():
        carry_scr[...] = jnp.zeros_like(carry_scr)

    hb = h_ref[...]
    row = lax.broadcasted_iota(I32, (SUBLANES, fc), 0)
    acc = jnp.zeros((tm, D_MODEL), F32)
    for cblk in range(D_FF // fc):
        halves = []
        for part in range(2):
            col = part * D_FF + cblk * fc
            u = jnp.dot(hb, wup_ref[:, col:col + fc], preferred_element_type=F32)
            prev = carry_scr[:, col:col + fc]
            r1 = pltpu.roll(u, 1, 0)
            r2 = pltpu.roll(u, 2, 0)
            head1 = jnp.where(row == 0, prev[7:8], r1[:SUBLANES])
            head2 = jnp.where(row == 0, prev[6:7], jnp.where(row == 1, prev[7:8], r2[:SUBLANES]))
            u1 = jnp.concatenate([head1, r1[SUBLANES:]], axis=0)
            u2 = jnp.concatenate([head2, r2[SUBLANES:]], axis=0)
            carry_scr[:, col:col + fc] = u[tm - SUBLANES:tm]
            cw = cw_ref[:, col:col + fc]
            halves.append(cb_ref[:, col:col + fc] + (cw[0:1] * u2 + cw[1:2] * u1 + cw[2:3] * u))
        act = (jax.nn.silu(halves[0]) * halves[1]).astype(BF16)
        acc = acc + jnp.dot(act, wdn_ref[cblk * fc:(cblk + 1) * fc, :],
                            preferred_element_type=F32)
    x2 = x_ref[...] + g2_ref[...] * acc
    if final:
        var = jnp.mean(x2 * x2, axis=-1, keepdims=True)
        x2 = x2 * lax.rsqrt(var + EPS) * fg_ref[...]
    o_ref[...] = x2


def _ffn_call(h2, x1, g2, wup, cw, cb, wdn, fg, *, tm, fc, final):
    B, S, _ = x1.shape
    row = lambda b, s: (b, s, 0)
    const2 = lambda b, s: (0, 0)
    return pl.pallas_call(
        functools.partial(_ffn_kernel, tm=tm, fc=fc, final=final),
        grid=(B, S // tm),
        in_specs=[
            pl.BlockSpec((None, tm, D_MODEL), row),
            pl.BlockSpec((None, tm, D_MODEL), row),
            pl.BlockSpec((None, 1, D_MODEL), lambda b, s: (b, 0, 0)),
            pl.BlockSpec((D_MODEL, 2 * D_FF), const2, pipeline_mode=pl.Buffered(1)),
            pl.BlockSpec((SUBLANES, 2 * D_FF), const2),
            pl.BlockSpec((1, 2 * D_FF), const2),
            pl.BlockSpec((D_FF, D_MODEL), const2, pipeline_mode=pl.Buffered(1)),
            pl.BlockSpec((1, D_MODEL), const2),
        ],
        out_specs=pl.BlockSpec((None, tm, D_MODEL), row),
        out_shape=jax.ShapeDtypeStruct((B, S, D_MODEL), F32),
        scratch_shapes=[pltpu.VMEM((SUBLANES, 2 * D_FF), F32)],
        compiler_params=_params(2),
        name="ffn",
    )(h2, x1, g2, wup, cw, cb, wdn, fg)


def _reorder_w_in(w):
    o = np.cumsum([0, W_HEADS, HEAD_DIM, HEAD_DIM, W_HEADS, HEAD_DIM, N_HEADS,
                   W_HEADS, W_HEADS, W_HEADS, N_HEADS, 2 * D_MODEL])
    seg = lambda k: w[:, o[k]:o[k + 1]]
    q_a, k_a, v_a, q_i, k_i, w_i, q_f, k_f, v_f, f_f, gates = [seg(k) for k in range(11)]
    pad = jnp.zeros((w.shape[0], N_COLS - int(o[-1])), w.dtype)
    return jnp.concatenate([q_a, q_i, q_f, k_f, v_f, gates, k_i, k_a, v_a, w_i, f_f, pad],
                           axis=1).astype(BF16)


def kernel(x, c, positions, mod_w, mod_b, norm1_g, norm2_g, w_in, forget_bias, w_branch_a,
           w_branch_b, w_out, w_up, conv_w, conv_b, w_down, final_g):
    B, S, _ = x.shape
    depth = mod_w.shape[0]
    tm = min(TOKEN_TILE, S)
    mod = _mod_call(c, mod_w, mod_b)[:, :, :B].reshape(depth, 6, B, 1, D_MODEL)
    pos = positions.astype(F32).reshape(B, S, 1)

    inv_freq = ROPE_THETA ** (-jnp.arange(0, ROT_DIM, 2, dtype=F32) / ROT_DIM)
    l64 = np.arange(LANES) % HEAD_DIM
    invf = jnp.where(l64 < ROT_DIM, inv_freq[l64 % (ROT_DIM // 2)], 0.0).reshape(1, LANES)

    place = np.zeros((N_PIECES, LANES, LANES), np.float32)
    for h in range(N_HEADS):
        for piece in range(N_PIECES):
            place[piece, LANE_FORGET + h, LANE_CK + N_PIECES * h + piece] = 1.0
    place = jnp.asarray(place, BF16)

    for l in range(depth):
        sh1, sc1, g1, sh2, sc2, g2 = [mod[l, k] for k in range(6)]
        fb = jnp.zeros((1, LANES), F32).at[0, LANE_FORGET:LANE_FORGET + N_HEADS].set(forget_bias[l])
        (qat, qit, qft, kf, vft, ga, gb, kk, vat, auxt) = _in_call(
            x, sh1, sc1, norm1_g[l].reshape(1, D_MODEL), pos, invf, fb, place,
            _reorder_w_in(w_in[l]), tm=tm)
        oa = _dsa_call(qit, qat, auxt, kk, vat, q_blk=DSA_Q_BLOCK, tk=min(DSA_KEY_TILE, S))
        ob = _fox_call(qft, kf, vft, auxt, t=min(FOX_TILE, S))
        x1, h2 = _merge_call(oa, ob, ga, gb, x, g1, sc2, sh2, norm2_g[l].reshape(1, D_MODEL),
                             w_branch_a[l].astype(BF16), w_branch_b[l].astype(BF16),
                             w_out[l].astype(BF16), tm=tm)
        cw8 = jnp.zeros((SUBLANES, 2 * D_FF), F32).at[:conv_w.shape[1]].set(conv_w[l])
        x = _ffn_call(h2, x1, g2, w_up[l].astype(BF16), cw8, conv_b[l].reshape(1, 2 * D_FF),
                      w_down[l].astype(BF16), final_g.reshape(1, D_MODEL),
                      tm=tm, fc=FFN_CHUNK, final=(l == depth - 1))
    return x
```
